```python
import math
import jax
import jax.numpy as jnp
from jax import lax
import numpy as np

D_MODEL = 2048
BATCH = 4
SEQ = 2048
DEPTH = 4
DEC_BATCH = 8
DEC_SEQ = 8
PAST_LEN = 16384
PAGE_SIZE = 128

N_MIXERS = 3
N_GLA = (DEPTH + 2) // 3
N_DIFF = (DEPTH + 1) // 3
N_GDN = DEPTH // 3
EPS = 1e-6
N_MOD = 9
D_FF = 5632

GLA_HEADS = 4
GLA_DK = D_MODEL // (2 * GLA_HEADS)
GLA_DV = D_MODEL // GLA_HEADS
GLA_KW = GLA_HEADS * GLA_DK
GLA_VW = GLA_HEADS * GLA_DV
GLA_RANK = 16
GLA_TAU = 16.0
GLA_CHUNK = 64

DIFF_DH = 128
DIFF_HEADS = D_MODEL // (2 * DIFF_DH)
DIFF_W = DIFF_HEADS * 2 * DIFF_DH
Q_BLOCK = 128

GDN_DK = 128
GDN_DV = 128
GDN_QK_HEADS = D_MODEL // GDN_DK
GDN_V_HEADS = 2 * GDN_QK_HEADS
GDN_KW = GDN_QK_HEADS * GDN_DK
GDN_VW = GDN_V_HEADS * GDN_DV
GDN_QKV_W = 2 * GDN_KW + GDN_VW
GDN_IN_W = GDN_QKV_W + GDN_VW + 2 * GDN_V_HEADS
GDN_CONV = 4
GDN_CHUNK = 64

kernel_name = 'hybrid_gla_diffattn_gdn_macaron_adaln_step'


def rmsnorm(x, gain):
    xf = x.astype(jnp.float32)
    y = xf * lax.rsqrt(jnp.mean(xf * xf, axis=-1, keepdims=True) + EPS)
    return (y * gain.astype(jnp.float32)).astype(x.dtype)


def modulate(h, shift, scale):
    return h * (1.0 + scale) + shift


def swiglu(h, w_up, w_down):
    gate, up = jnp.split(h @ w_up, 2, axis=-1)
    return (jax.nn.silu(gate) * up) @ w_down


def l2norm(t):
    t = t.astype(jnp.float32)
    return t * lax.rsqrt(jnp.sum(t * t, axis=-1, keepdims=True) + EPS)


def to_chunks(t, chunk):
    b, length = t.shape[:2]
    n = -(-length // chunk)
    t = jnp.pad(t.astype(jnp.float32), [(0, 0), (0, n * chunk - length)] + [(0, 0)] * (t.ndim - 2))
    t = t.reshape(b, n, chunk, *t.shape[2:])
    return jnp.moveaxis(t, (1, 3), (0, 2))


def from_chunks(t, length):
    t = jnp.moveaxis(t, (0, 2), (1, 3))
    return t.reshape(t.shape[0], -1, *t.shape[3:])[:, :length]


def gla_scan(q, k, v, log_a, s0):
    length = q.shape[1]
    chunk = min(GLA_CHUNK, length)
    causal = jnp.tril(jnp.ones((chunk, chunk), dtype=bool))[:, :, None]

    def step(s, blk):
        qc, kc, vc, ac = blk
        b = jnp.cumsum(ac, axis=2)
        rel = jnp.where(causal, b[:, :, :, None, :] - b[:, :, None, :, :], -jnp.inf)
        scores = jnp.einsum('bhid,bhjd,bhijd->bhij', qc, kc, jnp.exp(rel))
        o = (jnp.einsum('bhij,bhjv->bhiv', scores, vc)
             + jnp.einsum('bhid,bhdv->bhiv', qc * jnp.exp(b), s))
        b_last = b[:, :, -1]
        s = (jnp.exp(b_last)[..., None] * s
             + jnp.einsum('bhjd,bhjv->bhdv', kc * jnp.exp(b_last[:, :, None] - b), vc))
        return s, o

    s, o = lax.scan(step, s0.astype(jnp.float32), tuple(to_chunks(t, chunk) for t in (q, k, v, log_a)))
    return from_chunks(o, length), s


def gla_mixer(h, s0, w_in, w_a1, w_a2, b_a, norm_gain, w_o):
    bsz, length, _ = h.shape
    q, k, v, g = jnp.split(h @ w_in, [GLA_KW, 2 * GLA_KW, 2 * GLA_KW + GLA_VW], axis=-1)
    q = q.reshape(bsz, length, GLA_HEADS, GLA_DK) * (GLA_DK ** -0.5)
    k = k.reshape(bsz, length, GLA_HEADS, GLA_DK)
    v = v.reshape(bsz, length, GLA_HEADS, GLA_DV)
    log_a = jax.nn.log_sigmoid(((h @ w_a1) @ w_a2 + b_a).astype(jnp.float32)) / GLA_TAU
    o, s = gla_scan(q, k, v, log_a.reshape(bsz, length, GLA_HEADS, GLA_DK), s0)
    o = rmsnorm(o, norm_gain).astype(h.dtype).reshape(bsz, length, GLA_VW) * jax.nn.silu(g)
    return o @ w_o, s.astype(h.dtype)


def diff_attend(q, k, v, lam, q_offset):
    bsz, lq = q.shape[:2]
    qb = min(Q_BLOCK, lq)
    nb = -(-lq // qb)
    qblk = jnp.pad(q, [(0, 0), (0, nb * qb - lq), (0, 0), (0, 0), (0, 0)])
    qblk = jnp.moveaxis(qblk.reshape(bsz, nb, qb, *q.shape[2:]), 1, 0)
    kpos = jnp.arange(k.shape[1])

    def block(args):
        qi, start = args
        s = jnp.einsum('bqhmd,bkhmd->bhmqk', qi, k).astype(jnp.float32) * (DIFF_DH ** -0.5)
        qpos = q_offset + start + jnp.arange(qb)
        s = jnp.where(kpos[None, :] <= qpos[:, None], s, -jnp.inf)
        p = jax.nn.softmax(s, axis=-1)
        a = p[:, :, 0] - lam * p[:, :, 1]
        return jnp.einsum('bhqk,bkhe->bqhe', a.astype(v.dtype), v)

    o = lax.map(block, (qblk, jnp.arange(nb) * qb))
    return jnp.moveaxis(o, 0, 1).reshape(bsz, nb * qb, *v.shape[2:])[:, :lq]


def diff_mixer(h, k_past, v_past, w_in, lam_vecs, norm_gain, w_o, lambda_init):
    bsz, length, _ = h.shape
    q, k, v = jnp.split(h @ w_in, 3, axis=-1)
    q = q.reshape(bsz, length, DIFF_HEADS, 2, DIFF_DH)
    k = k.reshape(bsz, length, DIFF_HEADS, 2 * DIFF_DH)
    v = v.reshape(bsz, length, DIFF_HEADS, 2 * DIFF_DH)
    lv = lam_vecs.astype(jnp.float32)
    lam = jnp.exp(jnp.sum(lv[0] * lv[1])) - jnp.exp(jnp.sum(lv[2] * lv[3])) + lambda_init
    if k_past is None:
        k_all, v_all, offset = k, v, 0
    else:
        k_all = jnp.concatenate([k_past.astype(k.dtype), k], axis=1)
        v_all = jnp.concatenate([v_past.astype(v.dtype), v], axis=1)
        offset = k_past.shape[1]
    o = diff_attend(q, k_all.reshape(*k_all.shape[:3], 2, DIFF_DH), v_all, lam, offset)
    o = rmsnorm(o, norm_gain) * (1.0 - lambda_init)
    return o.reshape(bsz, length, DIFF_W) @ w_o, k, v


def gather_pages(pool, layer, page_table):
    rows = pool[layer, page_table]
    return rows.reshape(page_table.shape[0], -1, *pool.shape[3:])


def causal_conv(x, buf, w):
    xx = jnp.concatenate([buf.astype(x.dtype), x], axis=1)
    y = lax.conv_general_dilated(xx, w[:, None, :].astype(x.dtype), window_strides=(1,), padding='VALID',
                                 dimension_numbers=('NWC', 'WIO', 'NWC'), feature_group_count=x.shape[-1])
    return jax.nn.silu(y), xx[:, xx.shape[1] - (GDN_CONV - 1):]


def gdn_scan(q, k, v, g, beta, s0):
    length = q.shape[1]
    chunk = min(GDN_CHUNK, length)
    lower = jnp.tril(jnp.ones((chunk, chunk), dtype=bool))
    strict = jnp.tril(jnp.ones((chunk, chunk), dtype=bool), -1)
    eye = jnp.eye(chunk, dtype=jnp.float32)

    def step(s, blk):
        qc, kc, vc, gc, bc = blk
        G = jnp.cumsum(gc, axis=-1)
        decay = jnp.exp(jnp.where(lower, G[..., :, None] - G[..., None, :], -jnp.inf))
        m = jnp.where(strict, bc[..., :, None] * jnp.einsum('bhid,bhjd->bhij', kc, kc) * decay, 0.0)
        rhs = jnp.concatenate([bc[..., None] * vc, (bc * jnp.exp(G))[..., None] * kc], axis=-1)
        sol = lax.linalg.triangular_solve(eye + m, rhs, left_side=True, lower=True, unit_diagonal=True)
        u, wk = sol[..., :GDN_DV], sol[..., GDN_DV:]
        delta = u - jnp.einsum('bhid,bhdv->bhiv', wk, s)
        attn = jnp.einsum('bhid,bhjd->bhij', qc, kc) * decay
        o = (jnp.einsum('bhid,bhdv->bhiv', qc * jnp.exp(G)[..., None], s)
             + jnp.einsum('bhij,bhjv->bhiv', attn, delta))
        g_last = G[..., -1]
        s = (jnp.exp(g_last)[..., None, None] * s
             + jnp.einsum('bhjd,bhjv->bhdv', kc * jnp.exp(g_last[..., None] - G)[..., None], delta))
        return s, o

    s, o = lax.scan(step, s0.astype(jnp.float32), tuple(to_chunks(t, chunk) for t in (q, k, v, g, beta)))
    return from_chunks(o, length), s


def gdn_mixer(h, s0, conv_buf, w_in, conv_w, a_log, dt_bias, norm_gain, w_o):
    bsz, length, _ = h.shape
    qkv, z, b, a = jnp.split(h @ w_in, [GDN_QKV_W, GDN_QKV_W + GDN_VW, GDN_QKV_W + GDN_VW + GDN_V_HEADS], axis=-1)
    qkv, new_buf = causal_conv(qkv, conv_buf, conv_w)
    q, k, v = jnp.split(qkv, [GDN_KW, 2 * GDN_KW], axis=-1)
    rep = GDN_V_HEADS // GDN_QK_HEADS
    q = jnp.repeat(l2norm(q.reshape(bsz, length, GDN_QK_HEADS, GDN_DK)), rep, axis=2) * (GDN_DK ** -0.5)
    k = jnp.repeat(l2norm(k.reshape(bsz, length, GDN_QK_HEADS, GDN_DK)), rep, axis=2)
    v = v.reshape(bsz, length, GDN_V_HEADS, GDN_DV)
    beta = jax.nn.sigmoid(b.astype(jnp.float32))
    g = -jnp.exp(a_log.astype(jnp.float32)) * jax.nn.softplus(a.astype(jnp.float32) + dt_bias.astype(jnp.float32))
    o, s = gdn_scan(q, k, v, g, beta, s0)
    o = rmsnorm(o, norm_gain).astype(h.dtype) * jax.nn.silu(z.reshape(bsz, length, GDN_V_HEADS, GDN_DV))
    return o.reshape(bsz, length, GDN_VW) @ w_o, s.astype(h.dtype), new_buf


def trunk(x, c, w, past):
    bsz = x.shape[0]
    new_gla, new_dk, new_dv, new_gs, new_gc = [], [], [], [], []
    for i in range(DEPTH):
        kind, j = i % N_MIXERS, i // N_MIXERS
        mod = jax.nn.silu(c) @ w['ada_w'][i] + w['ada_b'][i]
        sh1, sc1, g1, sh2, sc2, g2, sh3, sc3, g3 = jnp.split(mod[:, None, :], N_MOD, axis=-1)
        h = modulate(rmsnorm(x, w['norm_gain'][i, 0]), sh1, sc1)
        x = x + 0.5 * g1 * swiglu(h, w['ffn_w_up'][i, 0], w['ffn_w_down'][i, 0])
        h = modulate(rmsnorm(x, w['norm_gain'][i, 1]), sh2, sc2)
        if kind == 0:
            if past is None:
                s0 = jnp.zeros((bsz, GLA_HEADS, GLA_DK, GLA_DV), x.dtype)
            else:
                s0 = past['state_gla'][j]
            y, s = gla_mixer(h, s0, w['gla_w_in'][j], w['gla_w_a1'][j], w['gla_w_a2'][j], w['gla_b_a'][j],
                             w['gla_norm'][j], w['gla_w_o'][j])
            new_gla.append(s)
        elif kind == 1:
            if past is None:
                k_past, v_past = None, None
            else:
                k_past = gather_pages(past['cache_diff_k'], j, past['page_table'])
                v_past = gather_pages(past['cache_diff_v'], j, past['page_table'])
            lambda_init = 0.8 - 0.6 * math.exp(-0.3 * i)
            y, k_new, v_new = diff_mixer(h, k_past, v_past, w['diff_w_in'][j], w['diff_lambda'][j],
                                         w['diff_norm'][j], w['diff_w_o'][j], lambda_init)
            new_dk.append(k_new)
            new_dv.append(v_new)
        else:
            if past is None:
                s0 = jnp.zeros((bsz, GDN_V_HEADS, GDN_DK, GDN_DV), x.dtype)
                buf = jnp.zeros((bsz, GDN_CONV - 1, GDN_QKV_W), x.dtype)
            else:
                s0 = past['state_gdn'][j]
                buf = past['state_gdn_conv'][j]
            y, s, nbuf = gdn_mixer(h, s0, buf, w['gdn_w_in'][j], w['gdn_conv_w'][j], w['gdn_a_log'][j],
                                   w['gdn_dt_bias'][j], w['gdn_norm'][j], w['gdn_w_o'][j])
            new_gs.append(s)
            new_gc.append(nbuf)
        x = x + g2 * y
        h = modulate(rmsnorm(x, w['norm_gain'][i, 2]), sh3, sc3)
        x = x + 0.5 * g3 * swiglu(h, w['ffn_w_up'][i, 1], w['ffn_w_down'][i, 1])
    return rmsnorm(x, w['final_gain']), (new_gla, new_dk, new_dv, new_gs, new_gc)


def setup_inputs(seed: int = 0) -> dict:
    key = jax.random.key(seed)
    ks = jax.random.split(key, 32)

    def nrm(i, shape, scale):
        return jax.random.normal(ks[i], shape, jnp.float32) * scale

    n_pages = PAST_LEN // PAGE_SIZE
    n_used = DEC_BATCH * n_pages
    n_pool = n_used + n_used // 4
    page_table = jax.random.permutation(ks[0], n_pool)[:n_used].reshape(DEC_BATCH, n_pages).astype(jnp.int32)
    kv_shape = (N_DIFF, n_pool, PAGE_SIZE, DIFF_HEADS, 2 * DIFF_DH)
    dt = jnp.exp(jax.random.uniform(ks[1], (N_GDN, GDN_V_HEADS), jnp.float32, math.log(1e-3), math.log(1e-1)))
    return {
        'x_prompt': nrm(2, (BATCH, SEQ, D_MODEL), 1.0),
        'x_sample': nrm(3, (DEC_BATCH, DEC_SEQ, D_MODEL), 1.0),
        'cache_diff_k': nrm(4, kv_shape, 1.0),
        'cache_diff_v': nrm(5, kv_shape, 1.0),
        'state_gla': nrm(6, (N_GLA, DEC_BATCH, GLA_HEADS, GLA_DK, GLA_DV), 0.5),
        'state_gdn': nrm(7, (N_GDN, DEC_BATCH, GDN_V_HEADS, GDN_DK, GDN_DV), 0.1),
        'state_gdn_conv': nrm(8, (N_GDN, DEC_BATCH, GDN_CONV - 1, GDN_QKV_W), 1.0),
        'page_table': page_table,
        'c_prompt': nrm(9, (BATCH, D_MODEL), 1.0),
        'c_sample': nrm(10, (DEC_BATCH, D_MODEL), 1.0),
        'ada_w': nrm(11, (DEPTH, D_MODEL, N_MOD * D_MODEL), 0.5 * D_MODEL ** -0.5),
        'ada_b': nrm(12, (DEPTH, N_MOD * D_MODEL), 0.01),
        'norm_gain': 1.0 + nrm(13, (DEPTH, 3, D_MODEL), 0.01),
        'final_gain': 1.0 + nrm(14, (D_MODEL,), 0.01),
        'ffn_w_up': nrm(15, (DEPTH, 2, D_MODEL, 2 * D_FF), D_MODEL ** -0.5),
        'ffn_w_down': nrm(16, (DEPTH, 2, D_FF, D_MODEL), D_FF ** -0.5),
        'gla_w_in': nrm(17, (N_GLA, D_MODEL, 2 * GLA_KW + 2 * GLA_VW), D_MODEL ** -0.5),
        'gla_w_a1': nrm(18, (N_GLA, D_MODEL, GLA_RANK), D_MODEL ** -0.5),
        'gla_w_a2': nrm(19, (N_GLA, GLA_RANK, GLA_KW), GLA_RANK ** -0.5),
        'gla_b_a': nrm(20, (N_GLA, GLA_KW), 0.1),
        'gla_norm': 1.0 + nrm(21, (N_GLA, GLA_DV), 0.01),
        'gla_w_o': nrm(22, (N_GLA, GLA_VW, D_MODEL), GLA_VW ** -0.5),
        'diff_w_in': nrm(23, (N_DIFF, D_MODEL, 3 * DIFF_W), D_MODEL ** -0.5),
        'diff_lambda': nrm(24, (N_DIFF, 4, DIFF_DH), 0.1),
        'diff_norm': 1.0 + nrm(25, (N_DIFF, 2 * DIFF_DH), 0.01),
        'diff_w_o': nrm(26, (N_DIFF, DIFF_W, D_MODEL), DIFF_W ** -0.5),
        'gdn_w_in': nrm(27, (N_GDN, D_MODEL, GDN_IN_W), D_MODEL ** -0.5),
        'gdn_conv_w': nrm(28, (N_GDN, GDN_CONV, GDN_QKV_W), GDN_CONV ** -0.5),
        'gdn_a_log': jnp.log(jax.random.uniform(ks[29], (N_GDN, GDN_V_HEADS), jnp.float32, 1.0, 16.0)),
        'gdn_dt_bias': dt + jnp.log(-jnp.expm1(-dt)),
        'gdn_norm': 1.0 + nrm(30, (N_GDN, GDN_DV), 0.01),
        'gdn_w_o': nrm(31, (N_GDN, GDN_VW, D_MODEL), GDN_VW ** -0.5),
    }


def reference(x_prompt, x_sample, cache_diff_k, cache_diff_v, state_gla, state_gdn, state_gdn_conv, page_table,
              c_prompt, c_sample, ada_w, ada_b, norm_gain, final_gain, ffn_w_up, ffn_w_down,
              gla_w_in, gla_w_a1, gla_w_a2, gla_b_a, gla_norm, gla_w_o,
              diff_w_in, diff_lambda, diff_norm, diff_w_o,
              gdn_w_in, gdn_conv_w, gdn_a_log, gdn_dt_bias, gdn_norm, gdn_w_o):
    w = {
        'ada_w': ada_w, 'ada_b': ada_b, 'norm_gain': norm_gain, 'final_gain': final_gain,
        'ffn_w_up': ffn_w_up, 'ffn_w_down': ffn_w_down,
        'gla_w_in': gla_w_in, 'gla_w_a1': gla_w_a1, 'gla_w_a2': gla_w_a2, 'gla_b_a': gla_b_a,
        'gla_norm': gla_norm, 'gla_w_o': gla_w_o,
        'diff_w_in': diff_w_in, 'diff_lambda': diff_lambda, 'diff_norm': diff_norm, 'diff_w_o': diff_w_o,
        'gdn_w_in': gdn_w_in, 'gdn_conv_w': gdn_conv_w, 'gdn_a_log': gdn_a_log, 'gdn_dt_bias': gdn_dt_bias,
        'gdn_norm': gdn_norm, 'gdn_w_o': gdn_w_o,
    }
    y_prompt, (gla_p, dk_p, dv_p, gdn_p, conv_p) = trunk(x_prompt, c_prompt, w, None)
    past = {
        'state_gla': state_gla, 'cache_diff_k': cache_diff_k, 'cache_diff_v': cache_diff_v,
        'page_table': page_table, 'state_gdn': state_gdn, 'state_gdn_conv': state_gdn_conv,
    }
    y_sample, (gla_s, dk_s, dv_s, gdn_s, conv_s) = trunk(x_sample, c_sample, w, past)
    return (y_prompt, y_sample,
            jnp.stack(gla_p), jnp.stack(gla_s),
            jnp.stack(dk_p), jnp.stack(dv_p), jnp.stack(dk_s), jnp.stack(dv_s),
            jnp.stack(gdn_p), jnp.stack(gdn_s),
            jnp.stack(conv_p), jnp.stack(conv_s))
```

```python
import functools
import math

import jax
import jax.numpy as jnp
from jax import lax
from jax.experimental import pallas as pl
from jax.experimental.pallas import tpu as pltpu

F32 = jnp.float32
BF16 = jnp.bfloat16
HIGHEST = lax.Precision.HIGHEST

D_MODEL = 2048
DEPTH = 4
EPS = 1e-6
N_MOD = 9
D_FF = 5632

GLA_HEADS = 4
GLA_DK = 256
GLA_DV = 512
GLA_KW = GLA_HEADS * GLA_DK
GLA_VW = GLA_HEADS * GLA_DV
GLA_RANK = 16
GLA_INV_TAU = 1.0 / 16.0
GLA_CHUNK = 64
GLA_SUB = 16

DIFF_DH = 128
DIFF_HEADS = 8
DIFF_W = 2048
PAGE_SIZE = 128

GDN_DK = 128
GDN_DV = 128
GDN_QK_HEADS = 16
GDN_V_HEADS = 32
GDN_KW = 2048
GDN_VW = 4096
GDN_QKV_W = 8192
GDN_CONV = 4
GDN_CHUNK = 64
GDN_GROUPS = 4
GDN_GQ = GDN_QK_HEADS // GDN_GROUPS
GDN_GV = GDN_V_HEADS // GDN_GROUPS

LANE = 128
VMEM_LIMIT = 56 * 1024 * 1024


def _params(*sem):
    return pltpu.CompilerParams(dimension_semantics=sem, vmem_limit_bytes=VMEM_LIMIT)


def _nt(a, b):
    return lax.dot_general(a, b, (((1,), (1,)), ((), ())), preferred_element_type=F32)


def _tn(a, b):
    return lax.dot_general(a, b, (((0,), (0,)), ((), ())), preferred_element_type=F32)


def _mm(a, b):
    return jnp.dot(a, b, preferred_element_type=F32)


def _mmh(a, b):
    return jnp.dot(a, b, preferred_element_type=F32, precision=HIGHEST)


def _silu(x):
    return x * jax.nn.sigmoid(x)


def _split2(x):
    hi = x.astype(BF16)
    lo = (x - hi.astype(F32)).astype(BF16)
    return hi, lo


def _ada_kernel(c_ref, w_ref, b_ref, o_ref):
    s = _silu(c_ref[...]).astype(BF16)
    o_ref[...] = _mm(s, w_ref[...].astype(BF16)) + b_ref[...]


def _ada(c_all, ada_w, ada_b):
    rows = c_all.shape[0]
    width = ada_w.shape[2]
    tn = 1024
    return pl.pallas_call(
        _ada_kernel,
        out_shape=jax.ShapeDtypeStruct((DEPTH, rows, width), F32),
        grid=(DEPTH, width // tn),
        in_specs=[
            pl.BlockSpec((rows, D_MODEL), lambda l, j: (0, 0)),
            pl.BlockSpec((None, D_MODEL, tn), lambda l, j: (l, 0, j)),
            pl.BlockSpec((None, 1, tn), lambda l, j: (l, 0, j)),
        ],
        out_specs=pl.BlockSpec((None, rows, tn), lambda l, j: (l, 0, j)),
        compiler_params=_params("parallel", "parallel"),
        name="ada_mod",
    )(c_all, ada_w, ada_b.reshape(DEPTH, 1, width))


def _linear_kernel(*refs, has_norm, n_seq, swiglu, has_resid, coef):
    it = iter(refs)
    x_ref = next(it)
    if has_norm:
        gain_ref, sh_ref, sc_ref = next(it), next(it), next(it)
    w_ref = next(it)
    wu_ref = next(it) if swiglu else None
    if has_resid:
        res_ref, gate_ref = next(it), next(it)
    o_ref = next(it)
    h_ref = next(it) if has_norm else None

    def per_seq(val, vec_ref, fn):
        if n_seq == 1:
            return fn(val, vec_ref[0])
        tm, n = val.shape
        return fn(val.reshape(n_seq, tm // n_seq, n), vec_ref[...]).reshape(tm, n)

    if has_norm:
        @pl.when(pl.program_id(1) == 0)
        def _():
            xf = x_ref[...].astype(F32)
            ms = jnp.mean(xf * xf, axis=-1, keepdims=True)
            y = xf * lax.rsqrt(ms + EPS) * gain_ref[...]
            y = per_seq(y, sc_ref, lambda a, s: a * (1.0 + s))
            y = per_seq(y, sh_ref, lambda a, s: a + s)
            h_ref[...] = y.astype(BF16)
        h = h_ref[...]
    else:
        h = x_ref[...].astype(BF16)

    acc = _mm(h, w_ref[...].astype(BF16))
    if swiglu:
        up = _mm(h, wu_ref[...].astype(BF16))
        acc = _silu(acc) * up
    if has_resid:
        acc = per_seq(acc, gate_ref, lambda a, g: a * (coef * g))
        acc = res_ref[...] + acc
    o_ref[...] = acc.astype(o_ref.dtype)


def _linear(x, w, w_prefix, *, n_out, tm, tn, out_dtype, seq=None, mod=None, layer=None,
            norm=None, swiglu=False, resid=None):
    m_rows, k_dim = x.shape
    tm = min(tm, m_rows)
    nj = n_out // tn
    grid = (m_rows // tm, nj)
    if seq is not None:
        seq0, seq_len = seq
        n_seq = max(1, tm // seq_len)
        tiles_per_seq = max(1, seq_len // tm)
        if n_seq == 1:
            seq_blk = lambda i: seq0 + i // tiles_per_seq
        else:
            seq_blk = lambda i: seq0 // n_seq + i
    else:
        n_seq = 1
    npre = len(w_prefix)

    def mod_spec(slot, width, per_j):
        blocks = D_MODEL // width
        if per_j:
            return pl.BlockSpec((None, n_seq, 1, width),
                                lambda i, j: (layer, seq_blk(i), 0, slot * blocks + j))
        return pl.BlockSpec((None, n_seq, 1, width),
                            lambda i, j: (layer, seq_blk(i), 0, slot * blocks))

    def w_spec(col0):
        return pl.BlockSpec((None,) * npre + (k_dim, tn),
                            lambda i, j: tuple(w_prefix) + (0, col0 + j))

    args = [x]
    in_specs = [pl.BlockSpec((tm, k_dim), lambda i, j: (i, 0))]
    scratch = []
    if norm is not None:
        gain, sh_slot, sc_slot = norm
        args += [gain, mod, mod]
        in_specs += [pl.BlockSpec((1, k_dim), lambda i, j: (0, 0)),
                     mod_spec(sh_slot, D_MODEL, False), mod_spec(sc_slot, D_MODEL, False)]
        scratch.append(pltpu.VMEM((tm, k_dim), BF16))
    args.append(w)
    in_specs.append(w_spec(0))
    if swiglu:
        args.append(w)
        in_specs.append(w_spec(nj))
    coef = 1.0
    if resid is not None:
        res, g_slot, coef = resid
        args += [res, mod]
        in_specs += [pl.BlockSpec((tm, tn), lambda i, j: (i, j)), mod_spec(g_slot, tn, True)]

    body = functools.partial(_linear_kernel, has_norm=norm is not None, n_seq=n_seq,
                             swiglu=swiglu, has_resid=resid is not None, coef=coef)
    return pl.pallas_call(
        body,
        out_shape=jax.ShapeDtypeStruct((m_rows, n_out), out_dtype),
        grid=grid,
        in_specs=in_specs,
        out_specs=pl.BlockSpec((tm, tn), lambda i, j: (i, j)),
        scratch_shapes=scratch,
        compiler_params=_params("parallel", "arbitrary"),
        name="linear",
    )(*args)


def _rmsnorm_kernel(x_ref, g_ref, o_ref):
    x = x_ref[...]
    ms = jnp.mean(x * x, axis=-1, keepdims=True)
    o_ref[...] = x * lax.rsqrt(ms + EPS) * g_ref[...]


def _final_norm(x, gain):
    m_rows = x.shape[0]
    tm = min(512, m_rows)
    return pl.pallas_call(
        _rmsnorm_kernel,
        out_shape=jax.ShapeDtypeStruct(x.shape, F32),
        grid=(m_rows // tm,),
        in_specs=[pl.BlockSpec((tm, D_MODEL), lambda i: (i, 0)),
                  pl.BlockSpec((1, D_MODEL), lambda i: (0, 0))],
        out_specs=pl.BlockSpec((tm, D_MODEL), lambda i: (i, 0)),
        compiler_params=_params("parallel"),
        name="final_norm",
    )(x, gain.reshape(1, D_MODEL))


def _gla_kernel(*refs, chunk, sub, n_chunks, has_state):
    if has_state:
        (q_ref, k_ref, v_ref, g_ref, r_ref, wa2_ref, ba_ref, gain_ref, s0_ref,
         o_ref, sout_ref, s_ref) = refs
    else:
        (q_ref, k_ref, v_ref, g_ref, r_ref, wa2_ref, ba_ref, gain_ref,
         o_ref, sout_ref, s_ref) = refs
    t = pl.program_id(2)
    mx = BF16 if chunk >= 16 else F32

    @pl.when(t == 0)
    def _():
        if has_state:
            s_ref[...] = s0_ref[...]
        else:
            s_ref[...] = jnp.zeros_like(s_ref)

    row = lax.broadcasted_iota(jnp.int32, (chunk, chunk), 0)
    col = lax.broadcasted_iota(jnp.int32, (chunk, chunk), 1)
    tri = (col <= row).astype(mx)
    rowi = lax.broadcasted_iota(jnp.int32, (chunk, 1), 0)
    ones = jnp.ones((chunk, LANE), mx)

    for c in range(n_chunks):
        r0 = c * chunk
        q = q_ref[r0:r0 + chunk, :] * (GLA_DK ** -0.5)
        k = k_ref[r0:r0 + chunk, :]
        v = v_ref[r0:r0 + chunk, :].astype(mx)
        z = _mm(r_ref[r0:r0 + chunk, :].astype(mx), wa2_ref[...].astype(mx)) + ba_ref[...]
        la = (jnp.minimum(z, 0.0) - jnp.log1p(jnp.exp(-jnp.abs(z)))) * GLA_INV_TAU
        if mx == BF16:
            la_hi, la_lo = _split2(la)
            b = _mm(tri, la_hi) + _mm(tri, la_lo)
            bcol = _tn(la_hi, ones) + _tn(la_lo, ones)
        else:
            b = _mmh(tri, la)
            bcol = lax.dot_general(la, ones, (((0,), (0,)), ((), ())),
                                   preferred_element_type=F32, precision=HIGHEST)
        s_old = s_ref[...]
        o = _mm((q * jnp.exp(b)).astype(mx), s_old.astype(mx))
        b_last = b[chunk - 1:chunk, :]
        kb = (k * jnp.exp(b_last - b)).astype(mx)

        parts = []
        for band in range(chunk // sub):
            lo, hi = band * sub, (band + 1) * sub
            mid = lo + sub // 2
            bref = b[mid:mid + 1, :]
            qi = (q[lo:hi] * jnp.exp(b[lo:hi] - bref)).astype(mx)
            e = jnp.where(rowi < hi, bref - b, 0.0)
            ki = (k * jnp.exp(e)).astype(mx)
            parts.append(_nt(qi, ki))
        sc = parts[0] if len(parts) == 1 else jnp.concatenate(parts, axis=0)
        sc = jnp.where(col <= row, sc, 0.0)
        o = o + _mm(sc.astype(mx), v)

        s_ref[...] = jnp.exp(bcol[:, 0:1]) * s_old + _tn(kb, v)

        ms = jnp.mean(o * o, axis=-1, keepdims=True)
        y = o * lax.rsqrt(ms + EPS) * gain_ref[...]
        o_ref[r0:r0 + chunk, :] = (y * _silu(g_ref[r0:r0 + chunk, :])).astype(o_ref.dtype)

    @pl.when(t == pl.num_programs(2) - 1)
    def _():
        sout_ref[...] = s_ref[...]


def _gla_core(proj, r, wa2p, b_a, gain, s0, *, bsz, length):
    chunk = min(GLA_CHUNK, length)
    sub = min(GLA_SUB, chunk)
    tl = min(256, length)
    grid = (bsz, GLA_HEADS, length // tl)
    in_specs = [
        pl.BlockSpec((None, tl, GLA_DK), lambda b, h, t: (b, t, h)),
        pl.BlockSpec((None, tl, GLA_DK), lambda b, h, t: (b, t, GLA_HEADS + h)),
        pl.BlockSpec((None, tl, GLA_DV), lambda b, h, t: (b, t, GLA_HEADS + h)),
        pl.BlockSpec((None, tl, GLA_DV), lambda b, h, t: (b, t, 2 * GLA_HEADS + h)),
        pl.BlockSpec((None, tl, LANE), lambda b, h, t: (b, t, 0)),
        pl.BlockSpec((LANE, GLA_DK), lambda b, h, t: (0, h)),
        pl.BlockSpec((1, GLA_DK), lambda b, h, t: (0, h)),
        pl.BlockSpec((1, GLA_DV), lambda b, h, t: (0, 0)),
    ]
    args = [proj, proj, proj, proj, r, wa2p, b_a, gain]
    if s0 is not None:
        in_specs.append(pl.BlockSpec((None, None, GLA_DK, GLA_DV), lambda b, h, t: (b, h, 0, 0)))
        args.append(s0)
    body = functools.partial(_gla_kernel, chunk=chunk, sub=sub, n_chunks=tl // chunk,
                             has_state=s0 is not None)
    return pl.pallas_call(
        body,
        out_shape=(jax.ShapeDtypeStruct((bsz, length, GLA_VW), BF16),
                   jax.ShapeDtypeStruct((bsz, GLA_HEADS, GLA_DK, GLA_DV), F32)),
        grid=grid,
        in_specs=in_specs,
        out_specs=(pl.BlockSpec((None, tl, GLA_DV), lambda b, h, t: (b, t, h)),
                   pl.BlockSpec((None, None, GLA_DK, GLA_DV), lambda b, h, t: (b, h, 0, 0))),
        scratch_shapes=[pltpu.VMEM((GLA_DK, GLA_DV), F32)],
        compiler_params=_params("parallel", "parallel", "arbitrary"),
        name="gla_scan",
    )(*args)


def _diff_lambda(lam_ref, lambda_init):
    lv = lam_ref[...]
    a = jnp.sum(lv[0:1] * lv[1:2], axis=-1, keepdims=True)
    b = jnp.sum(lv[2:3] * lv[3:4], axis=-1, keepdims=True)
    return jnp.exp(a) - jnp.exp(b) + lambda_init


def _diff_finish(o, gain_ref, lambda_init):
    ms = jnp.mean(o * o, axis=-1, keepdims=True)
    return o * lax.rsqrt(ms + EPS) * gain_ref[...] * (1.0 - lambda_init)


def _diff_prompt_kernel(q_ref, k_ref, v_ref, lam_ref, gain_ref, o_ref, m_ref, l_ref, acc_ref,
                        *, tq, tk, lambda_init):
    qi = pl.program_id(2)
    kj = pl.program_id(3)

    @pl.when(kj == 0)
    def _():
        m_ref[...] = jnp.full_like(m_ref, -jnp.inf)
        l_ref[...] = jnp.zeros_like(l_ref)
        acc_ref[...] = jnp.zeros_like(acc_ref)

    @pl.when(kj <= qi)
    def _():
        q = q_ref[...] * (DIFF_DH ** -0.5)
        k = k_ref[...]
        v = v_ref[...].astype(BF16)
        qpos = qi * tq + lax.broadcasted_iota(jnp.int32, (tq, tk), 0)
        kpos = kj * tk + lax.broadcasted_iota(jnp.int32, (tq, tk), 1)
        keep = kpos <= qpos
        for m in range(2):
            sl = slice(m * DIFF_DH, (m + 1) * DIFF_DH)
            s = _nt(q[:, sl].astype(BF16), k[:, sl].astype(BF16))
            s = jnp.where(keep, s, -jnp.inf)
            m_old = m_ref[m]
            m_new = jnp.maximum(m_old, jnp.max(s, axis=-1, keepdims=True))
            alpha = jnp.exp(m_old - m_new)
            p = jnp.exp(s - m_new)
            l_ref[m] = alpha * l_ref[m] + jnp.sum(p, axis=-1, keepdims=True)
            acc_ref[m] = alpha * acc_ref[m] + _mm(p.astype(BF16), v)
            m_ref[m] = m_new

    @pl.when(kj == qi)
    def _():
        lam = _diff_lambda(lam_ref, lambda_init)
        o = acc_ref[0] / l_ref[0] - lam * (acc_ref[1] / l_ref[1])
        o_ref[...] = _diff_finish(o, gain_ref, lambda_init).astype(o_ref.dtype)


def _diff_prompt(proj, lam_vecs, gain, *, bsz, length, lambda_init):
    tq = tk = min(512, length)
    nq = length // tq
    hw = 2 * DIFF_DH
    body = functools.partial(_diff_prompt_kernel, tq=tq, tk=tk, lambda_init=lambda_init)
    return pl.pallas_call(
        body,
        out_shape=jax.ShapeDtypeStruct((bsz, length, DIFF_W), BF16),
        grid=(bsz, DIFF_HEADS, nq, nq),
        in_specs=[
            pl.BlockSpec((None, tq, hw), lambda b, h, i, j: (b, i, h)),
            pl.BlockSpec((None, tk, hw), lambda b, h, i, j: (b, jnp.minimum(j, i), DIFF_HEADS + h)),
            pl.BlockSpec((None, tk, hw),
                         lambda b, h, i, j: (b, jnp.minimum(j, i), 2 * DIFF_HEADS + h)),
            pl.BlockSpec((4, DIFF_DH), lambda b, h, i, j: (0, 0)),
            pl.BlockSpec((1, hw), lambda b, h, i, j: (0, 0)),
        ],
        out_specs=pl.BlockSpec((None, tq, hw), lambda b, h, i, j: (b, i, h)),
        scratch_shapes=[pltpu.VMEM((2, tq, 1), F32), pltpu.VMEM((2, tq, 1), F32),
                        pltpu.VMEM((2, tq, hw), F32)],
        compiler_params=_params("parallel", "parallel", "parallel", "arbitrary"),
        name="diff_attn_prompt",
    )(proj, proj, proj, lam_vecs, gain)


def _diff_decode_kernel(pt_ref, q_ref, kc_ref, vc_ref, kn_ref, vn_ref, lam_ref, gain_ref, o_ref,
                        qt_ref, bias_ref, m_ref, l_ref, acc_ref, *, n_pages, n_new, lambda_init):
    del pt_ref
    b = pl.program_id(0)
    j = pl.program_id(1)
    nrow = 2 * DIFF_HEADS * n_new
    hw = 2 * DIFF_DH
    page_cols = PAGE_SIZE * DIFF_HEADS

    def head_match(ncols):
        r = lax.broadcasted_iota(jnp.int32, (nrow, ncols), 0)
        c = lax.broadcasted_iota(jnp.int32, (nrow, ncols), 1)
        return r, c, (c % DIFF_HEADS) == ((r // n_new) % DIFF_HEADS)

    @pl.when((b == 0) & (j == 0))
    def _():
        _, _, same = head_match(page_cols)
        bias_ref[...] = jnp.where(same, 0.0, -jnp.inf)

    @pl.when(j == 0)
    def _():
        qt_ref[...] = jnp.zeros_like(qt_ref)
        for m in range(2):
            for h in range(DIFF_HEADS):
                r0 = (m * DIFF_HEADS + h) * n_new
                c0 = h * hw + m * DIFF_DH
                qt_ref[r0:r0 + n_new, m * DIFF_DH:(m + 1) * DIFF_DH] = (
                    q_ref[:, c0:c0 + DIFF_DH] * (DIFF_DH ** -0.5))
        m_ref[...] = jnp.full_like(m_ref, -jnp.inf)
        l_ref[...] = jnp.zeros_like(l_ref)
        acc_ref[...] = jnp.zeros_like(acc_ref)

    def attend(k2d, v2d, bias):
        s = _nt(qt_ref[...].astype(BF16), k2d.astype(BF16)) + bias
        m_old = m_ref[...]
        m_new = jnp.maximum(m_old, jnp.max(s, axis=-1, keepdims=True))
        alpha = jnp.exp(m_old - m_new)
        p = jnp.exp(s - m_new)
        l_ref[...] = alpha * l_ref[...] + jnp.sum(p, axis=-1, keepdims=True)
        acc_ref[...] = alpha * acc_ref[...] + _mm(p.astype(BF16), v2d.astype(BF16))
        m_ref[...] = m_new

    @pl.when(j < n_pages)
    def _():
        attend(kc_ref[...].reshape(page_cols, hw), vc_ref[...].reshape(page_cols, hw),
               bias_ref[...])

    @pl.when(j == n_pages)
    def _():
        r, c, same = head_match(n_new * DIFF_HEADS)
        keep = same & ((c // DIFF_HEADS) <= (r % n_new))
        attend(kn_ref[...], vn_ref[...], jnp.where(keep, 0.0, -jnp.inf))
        lam = _diff_lambda(lam_ref, lambda_init)
        o2 = acc_ref[...] / l_ref[...]
        half = nrow // 2
        o = o2[0:half] - lam * o2[half:nrow]
        y = _diff_finish(o, gain_ref, lambda_init)
        for h in range(DIFF_HEADS):
            o_ref[:, h * hw:(h + 1) * hw] = y[h * n_new:(h + 1) * n_new, :].astype(o_ref.dtype)


def _diff_decode(q, k_new, v_new, cache_k, cache_v, page_table, layer, lam_vecs, gain,
                 *, bsz, n_new, lambda_init):
    n_pages = page_table.shape[1]
    hw = 2 * DIFF_DH
    nrow = 2 * DIFF_HEADS * n_new
    last = n_pages - 1
    page_blk = (None, None, PAGE_SIZE, DIFF_HEADS, hw)
    body = functools.partial(_diff_decode_kernel, n_pages=n_pages, n_new=n_new,
                             lambda_init=lambda_init)
    grid_spec = pltpu.PrefetchScalarGridSpec(
        num_scalar_prefetch=1,
        grid=(bsz, n_pages + 1),
        in_specs=[
            pl.BlockSpec((None, n_new, DIFF_W), lambda b, j, pt: (b, 0, 0)),
            pl.BlockSpec(page_blk, lambda b, j, pt: (layer, pt[b, jnp.minimum(j, last)], 0, 0, 0)),
            pl.BlockSpec(page_blk, lambda b, j, pt: (layer, pt[b, jnp.minimum(j, last)], 0, 0, 0)),
            pl.BlockSpec((None, n_new * DIFF_HEADS, hw), lambda b, j, pt: (b, 0, 0)),
            pl.BlockSpec((None, n_new * DIFF_HEADS, hw), lambda b, j, pt: (b, 0, 0)),
            pl.BlockSpec((4, DIFF_DH), lambda b, j, pt: (0, 0)),
            pl.BlockSpec((1, hw), lambda b, j, pt: (0, 0)),
        ],
        out_specs=pl.BlockSpec((None, n_new, DIFF_W), lambda b, j, pt: (b, 0, 0)),
        scratch_shapes=[
            pltpu.VMEM((nrow, hw), F32),
            pltpu.VMEM((nrow, PAGE_SIZE * DIFF_HEADS), F32),
            pltpu.VMEM((nrow, 1), F32),
            pltpu.VMEM((nrow, 1), F32),
            pltpu.VMEM((nrow, hw), F32),
        ],
    )
    return pl.pallas_call(
        body,
        out_shape=jax.ShapeDtypeStruct((bsz, n_new, DIFF_W), BF16),
        grid_spec=grid_spec,
        compiler_params=_params("arbitrary", "arbitrary"),
        name="diff_attn_decode",
    )(page_table, q, cache_k, cache_v, k_new, v_new, lam_vecs, gain)


def _gdn_prep_kernel(x_ref, buf_ref, w_ref, o_ref, nb_ref, *, length, tc, nq_blocks, nqk_blocks):
    j = pl.program_id(1)
    x = x_ref[...]
    buf = buf_ref[...]
    w = w_ref[...]
    row = lax.broadcasted_iota(jnp.int32, (length, 1), 0)
    b0, b1, b2 = buf[0:1], buf[1:2], buf[2:3]
    x1 = jnp.where(row >= 1, pltpu.roll(x, 1, 0), b2)
    x2 = jnp.where(row >= 2, pltpu.roll(x, 2, 0), jnp.where(row == 1, b2, b1))
    x3 = jnp.where(row >= 3, pltpu.roll(x, 3, 0),
                   jnp.where(row == 2, b2, jnp.where(row == 1, b1, b0)))
    y = w[3:4] * x + w[2:3] * x1 + w[1:2] * x2 + w[0:1] * x3
    y = _silu(y)
    nb_ref[...] = x[length - (GDN_CONV - 1):length, :]

    @pl.when(j < nqk_blocks)
    def _():
        scale = jnp.where(j < nq_blocks, GDN_DK ** -0.5, 1.0)
        for h in range(tc // GDN_DK):
            sl = slice(h * GDN_DK, (h + 1) * GDN_DK)
            yh = y[:, sl]
            ss = jnp.sum(yh * yh, axis=-1, keepdims=True)
            o_ref[:, sl] = yh * (lax.rsqrt(ss + EPS) * scale)

    @pl.when(j >= nqk_blocks)
    def _():
        o_ref[...] = y


def _gdn_prep(proj, buf, conv_w, *, bsz, length):
    tc = 512
    body = functools.partial(_gdn_prep_kernel, length=length, tc=tc,
                             nq_blocks=GDN_KW // tc, nqk_blocks=2 * GDN_KW // tc)
    return pl.pallas_call(
        body,
        out_shape=(jax.ShapeDtypeStruct((bsz, length, GDN_QKV_W), F32),
                   jax.ShapeDtypeStruct((bsz, GDN_CONV - 1, GDN_QKV_W), F32)),
        grid=(bsz, GDN_QKV_W // tc),
        in_specs=[
            pl.BlockSpec((None, length, tc), lambda b, j: (b, 0, j)),
            pl.BlockSpec((None, GDN_CONV - 1, tc), lambda b, j: (b, 0, j)),
            pl.BlockSpec((GDN_CONV, tc), lambda b, j: (0, j)),
        ],
        out_specs=(pl.BlockSpec((None, length, tc), lambda b, j: (b, 0, j)),
                   pl.BlockSpec((None, GDN_CONV - 1, tc), lambda b, j: (b, 0, j))),
        compiler_params=_params("parallel", "parallel"),
        name="gdn_conv_prep",
    )(proj, buf, conv_w)


def _gdn_kernel(*refs, chunk, has_state):
    if has_state:
        (q_ref, k_ref, v_ref, z_ref, bg_ref, alog_ref, dtb_ref, gain_ref, s0_ref,
         o_ref, sout_ref, s_ref) = refs
    else:
        (q_ref, k_ref, v_ref, z_ref, bg_ref, alog_ref, dtb_ref, gain_ref,
         o_ref, sout_ref, s_ref) = refs
    t = pl.program_id(2)
    mx = BF16 if chunk >= 16 else F32

    @pl.when(t == 0)
    def _():
        if has_state:
            s_ref[...] = s0_ref[...]
        else:
            s_ref[...] = jnp.zeros_like(s_ref)

    row = lax.broadcasted_iota(jnp.int32, (chunk, chunk), 0)
    col = lax.broadcasted_iota(jnp.int32, (chunk, chunk), 1)
    lower = col <= row
    strict = col < row
    eye = col == row
    eye_f = eye.astype(F32)
    ones_cc = jnp.ones((chunk, chunk), F32)

    bg = bg_ref[...]
    beta_all = jax.nn.sigmoid(bg[:, 0:GDN_GV])
    a_in = bg[:, GDN_GV:2 * GDN_GV] + dtb_ref[...]
    softplus = jnp.maximum(a_in, 0.0) + jnp.log1p(jnp.exp(-jnp.abs(a_in)))
    g_all = -jnp.exp(alog_ref[...]) * softplus
    gc_all = _mmh(lower.astype(F32), g_all)

    for hq in range(GDN_GQ):
        sl = slice(hq * GDN_DK, (hq + 1) * GDN_DK)
        q = q_ref[:, sl]
        k = k_ref[:, sl]
        kk = _nt(k.astype(mx), k.astype(mx))
        qk = _nt(q.astype(mx), k.astype(mx))
        for rep in range(GDN_GV // GDN_GQ):
            hv = hq * (GDN_GV // GDN_GQ) + rep
            vs = slice(hv * GDN_DV, (hv + 1) * GDN_DV)
            v = v_ref[:, vs]
            gcol = gc_all[:, hv:hv + 1]
            beta = beta_all[:, hv:hv + 1]
            grow = _mmh(ones_cc, jnp.where(eye, gcol, 0.0))
            decay = jnp.exp(jnp.where(lower, gcol - grow, -jnp.inf))
            m = jnp.where(strict, beta * kk * decay, 0.0)
            inv = eye_f
            size = 1
            while size < chunk:
                blk = ((row // (2 * size)) == (col // (2 * size))) & \
                      ((row // size) % 2 == 1) & ((col // size) % 2 == 0)
                low = jnp.where(blk, m, 0.0)
                inv = inv - _mmh(_mmh(inv, low), inv)
                size *= 2
            eg = jnp.exp(gcol)
            rhs = jnp.concatenate([beta * v, (beta * eg) * k], axis=-1)
            sol = _mmh(inv, rhs)
            u, wk = sol[:, 0:GDN_DV], sol[:, GDN_DV:]
            s_old = s_ref[hv]
            s_mx = s_old.astype(mx)
            delta = u - _mm(wk.astype(mx), s_mx)
            attn = jnp.where(lower, qk * decay, 0.0)
            o = _mm((q * eg).astype(mx), s_mx) + _mm(attn.astype(mx), delta.astype(mx))
            g_last = gcol[chunk - 1:chunk, :]
            kd = (k * jnp.exp(g_last - gcol)).astype(mx)
            s_ref[hv] = jnp.exp(g_last) * s_old + _tn(kd, delta.astype(mx))
            ms = jnp.mean(o * o, axis=-1, keepdims=True)
            y = o * lax.rsqrt(ms + EPS) * gain_ref[...]
            o_ref[:, vs] = (y * _silu(z_ref[:, vs])).astype(o_ref.dtype)

    @pl.when(t == pl.num_programs(2) - 1)
    def _():
        sout_ref[...] = s_ref[...]


def _gdn_core(qkv, proj, bg, a_log, dt_bias, gain, s0, *, bsz, length):
    chunk = min(GDN_CHUNK, length)
    qw = GDN_GQ * GDN_DK
    vw = GDN_GV * GDN_DV
    in_specs = [
        pl.BlockSpec((None, chunk, qw), lambda b, g, t: (b, t, g)),
        pl.BlockSpec((None, chunk, qw), lambda b, g, t: (b, t, GDN_KW // qw + g)),
        pl.BlockSpec((None, chunk, vw), lambda b, g, t: (b, t, 2 * GDN_KW // vw + g)),
        pl.BlockSpec((None, chunk, vw), lambda b, g, t: (b, t, GDN_QKV_W // vw + g)),
        pl.BlockSpec((None, None, chunk, 2 * GDN_GV), lambda b, g, t: (b, g, t, 0)),
        pl.BlockSpec((None, 1, GDN_GV), lambda b, g, t: (g, 0, 0)),
        pl.BlockSpec((None, 1, GDN_GV), lambda b, g, t: (g, 0, 0)),
        pl.BlockSpec((1, GDN_DV), lambda b, g, t: (0, 0)),
    ]
    args = [qkv, qkv, qkv, proj, bg, a_log, dt_bias, gain]
    if s0 is not None:
        in_specs.append(pl.BlockSpec((None, GDN_GV, GDN_DK, GDN_DV), lambda b, g, t: (b, g, 0, 0)))
        args.append(s0)
    body = functools.partial(_gdn_kernel, chunk=chunk, has_state=s0 is not None)
    return pl.pallas_call(
        body,
        out_shape=(jax.ShapeDtypeStruct((bsz, length, GDN_VW), BF16),
                   jax.ShapeDtypeStruct((bsz, GDN_V_HEADS, GDN_DK, GDN_DV), F32)),
        grid=(bsz, GDN_GROUPS, length // chunk),
        in_specs=in_specs,
        out_specs=(pl.BlockSpec((None, chunk, vw), lambda b, g, t: (b, t, g)),
                   pl.BlockSpec((None, GDN_GV, GDN_DK, GDN_DV), lambda b, g, t: (b, g, 0, 0))),
        scratch_shapes=[pltpu.VMEM((GDN_GV, GDN_DK, GDN_DV), F32)],
        compiler_params=_params("parallel", "parallel", "arbitrary"),
        name="gdn_scan",
    )(*args)


def _trunk(x, seq0, w, past, mod):
    bsz, length, _ = x.shape
    m_rows = bsz * length
    x = x.reshape(m_rows, D_MODEL)
    seq = (seq0, length)
    lin = functools.partial(_linear, seq=seq, mod=mod, tm=1024)
    new_gla, new_dk, new_dv, new_gs, new_gc = [], [], [], [], []

    def ffn(x, i, which, slots):
        sh, sc, g = slots
        gain = w['norm_gain'][i, 2 * which].reshape(1, D_MODEL)
        act = lin(x, w['ffn_w_up'], (i, which), n_out=D_FF, tn=512, out_dtype=BF16, layer=i,
                  norm=(gain, sh, sc), swiglu=True)
        return lin(act, w['ffn_w_down'], (i, which), n_out=D_MODEL, tn=256, out_dtype=F32,
                   layer=i, resid=(x, g, 0.5))

    for i in range(DEPTH):
        kind, j = i % 3, i // 3
        x = ffn(x, i, 0, (0, 1, 2))
        gain2 = w['norm_gain'][i, 1].reshape(1, D_MODEL)
        norm2 = (gain2, 3, 4)
        if kind == 0:
            proj = lin(x, w['gla_w_in'], (j,), n_out=2 * GLA_KW + 2 * GLA_VW, tn=512,
                       out_dtype=F32, layer=i, norm=norm2)
            wa1p = jnp.pad(w['gla_w_a1'][j], ((0, 0), (0, LANE - GLA_RANK)))
            r = lin(x, wa1p, (), n_out=LANE, tn=LANE, out_dtype=F32, layer=i, norm=norm2)
            wa2p = jnp.pad(w['gla_w_a2'][j], ((0, LANE - GLA_RANK), (0, 0)))
            s0 = None if past is None else past['state_gla'][j]
            o, s = _gla_core(proj.reshape(bsz, length, -1), r.reshape(bsz, length, LANE), wa2p,
                             w['gla_b_a'][j].reshape(1, GLA_KW), w['gla_norm'][j].reshape(1, GLA_DV),
                             s0, bsz=bsz, length=length)
            new_gla.append(s)
            w_o, w_o_pre = w['gla_w_o'], (j,)
        elif kind == 1:
            lambda_init = 0.8 - 0.6 * math.exp(-0.3 * i)
            proj = lin(x, w['diff_w_in'], (j,), n_out=3 * DIFF_W, tn=512, out_dtype=F32,
                       layer=i, norm=norm2)
            proj3 = proj.reshape(bsz, length, 3 * DIFF_W)
            k_new = proj3[:, :, DIFF_W:2 * DIFF_W].reshape(bsz, length, DIFF_HEADS, 2 * DIFF_DH)
            v_new = proj3[:, :, 2 * DIFF_W:].reshape(bsz, length, DIFF_HEADS, 2 * DIFF_DH)
            gain = w['diff_norm'][j].reshape(1, 2 * DIFF_DH)
            if past is None:
                o = _diff_prompt(proj3, w['diff_lambda'][j], gain, bsz=bsz, length=length,
                                 lambda_init=lambda_init)
            else:
                o = _diff_decode(proj3[:, :, 0:DIFF_W],
                                 k_new.reshape(bsz, length * DIFF_HEADS, 2 * DIFF_DH),
                                 v_new.reshape(bsz, length * DIFF_HEADS, 2 * DIFF_DH),
                                 past['cache_diff_k'], past['cache_diff_v'], past['page_table'], j,
                                 w['diff_lambda'][j], gain, bsz=bsz, n_new=length,
                                 lambda_init=lambda_init)
            new_dk.append(k_new)
            new_dv.append(v_new)
            w_o, w_o_pre = w['diff_w_o'], (j,)
        else:
            main_w = GDN_QKV_W + GDN_VW
            proj = lin(x, w['gdn_w_in'], (j,), n_out=main_w, tn=512, out_dtype=F32, layer=i,
                       norm=norm2)
            w_ba = jnp.pad(w['gdn_w_in'][j][:, main_w:], ((0, 0), (0, LANE - 2 * GDN_V_HEADS)))
            ba = lin(x, w_ba, (), n_out=LANE, tn=LANE, out_dtype=F32, layer=i, norm=norm2)
            ba = ba.reshape(bsz, length, LANE)[:, :, 0:2 * GDN_V_HEADS]
            bg = ba.reshape(bsz, length, 2, GDN_GROUPS, GDN_GV)
            bg = jnp.transpose(bg, (0, 3, 1, 2, 4)).reshape(bsz, GDN_GROUPS, length, 2 * GDN_GV)
            proj3 = proj.reshape(bsz, length, main_w)
            if past is None:
                buf = jnp.zeros((bsz, GDN_CONV - 1, GDN_QKV_W), F32)
                s0 = None
            else:
                buf = past['state_gdn_conv'][j]
                s0 = past['state_gdn'][j]
            qkv, nbuf = _gdn_prep(proj3, buf, w['gdn_conv_w'][j], bsz=bsz, length=length)
            o, s = _gdn_core(qkv, proj3, bg,
                             w['gdn_a_log'][j].reshape(GDN_GROUPS, 1, GDN_GV),
                             w['gdn_dt_bias'][j].reshape(GDN_GROUPS, 1, GDN_GV),
                             w['gdn_norm'][j].reshape(1, GDN_DV), s0, bsz=bsz, length=length)
            new_gs.append(s)
            new_gc.append(nbuf)
            w_o, w_o_pre = w['gdn_w_o'], (j,)
        o2 = o.reshape(m_rows, -1)
        x = lin(o2, w_o, w_o_pre, n_out=D_MODEL, tn=512, out_dtype=F32, layer=i,
                resid=(x, 5, 1.0))
        x = ffn(x, i, 1, (6, 7, 8))
    y = _final_norm(x, w['final_gain']).reshape(bsz, length, D_MODEL)
    return y, (new_gla, new_dk, new_dv, new_gs, new_gc)


def kernel(x_prompt, x_sample, cache_diff_k, cache_diff_v, state_gla, state_gdn, state_gdn_conv,
           page_table, c_prompt, c_sample, ada_w, ada_b, norm_gain, final_gain, ffn_w_up,
           ffn_w_down, gla_w_in, gla_w_a1, gla_w_a2, gla_b_a, gla_norm, gla_w_o, diff_w_in,
           diff_lambda, diff_norm, diff_w_o, gdn_w_in, gdn_conv_w, gdn_a_log, gdn_dt_bias,
           gdn_norm, gdn_w_o):
    w = {
        'norm_gain': norm_gain, 'final_gain': final_gain,
        'ffn_w_up': ffn_w_up, 'ffn_w_down': ffn_w_down,
        'gla_w_in': gla_w_in, 'gla_w_a1': gla_w_a1, 'gla_w_a2': gla_w_a2, 'gla_b_a': gla_b_a,
        'gla_norm': gla_norm, 'gla_w_o': gla_w_o,
        'diff_w_in': diff_w_in, 'diff_lambda': diff_lambda, 'diff_norm': diff_norm,
        'diff_w_o': diff_w_o,
        'gdn_w_in': gdn_w_in, 'gdn_conv_w': gdn_conv_w, 'gdn_a_log': gdn_a_log,
        'gdn_dt_bias': gdn_dt_bias, 'gdn_norm': gdn_norm, 'gdn_w_o': gdn_w_o,
    }
    n_s, n_p = c_sample.shape[0], c_prompt.shape[0]
    n_rows = 16
    c_all = jnp.concatenate(
        [c_sample, c_prompt, jnp.zeros((n_rows - n_s - n_p, D_MODEL), F32)], axis=0)
    mod = _ada(c_all, ada_w, ada_b).reshape(DEPTH, n_rows, 1, N_MOD * D_MODEL)

    y_p, (gla_p, dk_p, dv_p, gdn_p, conv_p) = _trunk(x_prompt, n_s, w, None, mod)
    past = {
        'state_gla': state_gla, 'cache_diff_k': cache_diff_k, 'cache_diff_v': cache_diff_v,
        'page_table': page_table, 'state_gdn': state_gdn, 'state_gdn_conv': state_gdn_conv,
    }
    y_s, (gla_s, dk_s, dv_s, gdn_s, conv_s) = _trunk(x_sample, 0, w, past, mod)
    return (y_p, y_s,
            jnp.stack(gla_p), jnp.stack(gla_s),
            jnp.stack(dk_p), jnp.stack(dv_p), jnp.stack(dk_s), jnp.stack(dv_s),
            jnp.stack(gdn_p), jnp.stack(gdn_s),
            jnp.stack(conv_p), jnp.stack(conv_s))
```

```python
import functools
import math

import jax
import jax.numpy as jnp
from jax import lax
from jax.experimental import pallas as pl
from jax.experimental.pallas import tpu as pltpu

F32 = jnp.float32
BF16 = jnp.bfloat16
HIGHEST = lax.Precision.HIGHEST

D_MODEL = 2048
DEPTH = 4
EPS = 1e-6
N_MOD = 9
D_FF = 5632

GLA_HEADS = 4
GLA_DK = 256
GLA_DV = 512
GLA_KW = GLA_HEADS * GLA_DK
GLA_VW = GLA_HEADS * GLA_DV
GLA_RANK = 16
GLA_INV_TAU = 1.0 / 16.0
GLA_CHUNK = 64
GLA_SUB = 16

DIFF_DH = 128
DIFF_HEADS = 8
DIFF_W = 2048
PAGE_SIZE = 128

GDN_DK = 128
GDN_DV = 128
GDN_QK_HEADS = 16
GDN_V_HEADS = 32
GDN_KW = 2048
GDN_VW = 4096
GDN_QKV_W = 8192
GDN_CONV = 4
GDN_CHUNK = 64
GDN_GROUPS = 4
GDN_GQ = GDN_QK_HEADS // GDN_GROUPS
GDN_GV = GDN_V_HEADS // GDN_GROUPS

LANE = 128
VMEM_LIMIT = 56 * 1024 * 1024


def _params(*sem):
    return pltpu.CompilerParams(dimension_semantics=sem, vmem_limit_bytes=VMEM_LIMIT)


def _nt(a, b):
    return lax.dot_general(a, b, (((1,), (1,)), ((), ())), preferred_element_type=F32)


def _tn(a, b):
    return lax.dot_general(a, b, (((0,), (0,)), ((), ())), preferred_element_type=F32)


def _mm(a, b):
    return jnp.dot(a, b, preferred_element_type=F32)


def _mmh(a, b):
    return jnp.dot(a, b, preferred_element_type=F32, precision=HIGHEST)


def _silu(x):
    return x * jax.nn.sigmoid(x)


def _split2(x):
    hi = x.astype(BF16)
    lo = (x - hi.astype(F32)).astype(BF16)
    return hi, lo


def _ada_kernel(c_ref, w_ref, b_ref, o_ref):
    s = _silu(c_ref[...]).astype(BF16)
    o_ref[...] = _mm(s, w_ref[...].astype(BF16)) + b_ref[...]


def _ada(c_all, ada_w, ada_b):
    rows = c_all.shape[0]
    width = ada_w.shape[2]
    tn = 1024
    return pl.pallas_call(
        _ada_kernel,
        out_shape=jax.ShapeDtypeStruct((DEPTH, rows, width), F32),
        grid=(DEPTH, width // tn),
        in_specs=[
            pl.BlockSpec((rows, D_MODEL), lambda l, j: (0, 0)),
            pl.BlockSpec((None, D_MODEL, tn), lambda l, j: (l, 0, j)),
            pl.BlockSpec((None, 1, tn), lambda l, j: (l, 0, j)),
        ],
        out_specs=pl.BlockSpec((None, rows, tn), lambda l, j: (l, 0, j)),
        compiler_params=_params("parallel", "parallel"),
        name="ada_mod",
    )(c_all, ada_w, ada_b.reshape(DEPTH, 1, width))


def _linear_kernel(*refs, has_norm, n_seq, swiglu, has_resid, coef):
    it = iter(refs)
    x_ref = next(it)
    if has_norm:
        gain_ref, sh_ref, sc_ref = next(it), next(it), next(it)
    w_ref = next(it)
    wu_ref = next(it) if swiglu else None
    if has_resid:
        res_ref, gate_ref = next(it), next(it)
    o_ref = next(it)
    h_ref = next(it) if has_norm else None

    def per_seq(val, vec_ref, fn):
        if n_seq == 1:
            return fn(val, vec_ref[0])
        tm, n = val.shape
        return fn(val.reshape(n_seq, tm // n_seq, n), vec_ref[...]).reshape(tm, n)

    if has_norm:
        @pl.when(pl.program_id(1) == 0)
        def _():
            xf = x_ref[...].astype(F32)
            ms = jnp.mean(xf * xf, axis=-1, keepdims=True)
            y = xf * lax.rsqrt(ms + EPS) * gain_ref[...]
            y = per_seq(y, sc_ref, lambda a, s: a * (1.0 + s))
            y = per_seq(y, sh_ref, lambda a, s: a + s)
            h_ref[...] = y.astype(BF16)
        h = h_ref[...]
    else:
        h = x_ref[...].astype(BF16)

    acc = _mm(h, w_ref[...].astype(BF16))
    if swiglu:
        up = _mm(h, wu_ref[...].astype(BF16))
        acc = _silu(acc) * up
    if has_resid:
        acc = per_seq(acc, gate_ref, lambda a, g: a * (coef * g))
        acc = res_ref[...] + acc
    o_ref[...] = acc.astype(o_ref.dtype)


def _linear(x, w, w_prefix, *, n_out, tm, tn, out_dtype, seq=None, mod=None, layer=None,
            norm=None, swiglu=False, resid=None):
    m_rows, k_dim = x.shape
    tm = min(tm, m_rows)
    nj = n_out // tn
    grid = (m_rows // tm, nj)
    if seq is not None:
        seq0, seq_len = seq
        n_seq = max(1, tm // seq_len)
        tiles_per_seq = max(1, seq_len // tm)
        if n_seq == 1:
            seq_blk = lambda i: seq0 + i // tiles_per_seq
        else:
            seq_blk = lambda i: seq0 // n_seq + i
    else:
        n_seq = 1
    npre = len(w_prefix)

    def mod_spec(slot, width, per_j):
        blocks = D_MODEL // width
        if per_j:
            return pl.BlockSpec((None, n_seq, 1, width),
                                lambda i, j: (layer, seq_blk(i), 0, slot * blocks + j))
        return pl.BlockSpec((None, n_seq, 1, width),
                            lambda i, j: (layer, seq_blk(i), 0, slot * blocks))

    def w_spec(col0):
        return pl.BlockSpec((None,) * npre + (k_dim, tn),
                            lambda i, j: tuple(w_prefix) + (0, col0 + j))

    args = [x]
    in_specs = [pl.BlockSpec((tm, k_dim), lambda i, j: (i, 0))]
    scratch = []
    if norm is not None:
        gain, sh_slot, sc_slot = norm
        args += [gain, mod, mod]
        in_specs += [pl.BlockSpec((1, k_dim), lambda i, j: (0, 0)),
                     mod_spec(sh_slot, D_MODEL, False), mod_spec(sc_slot, D_MODEL, False)]
        scratch.append(pltpu.VMEM((tm, k_dim), BF16))
    args.append(w)
    in_specs.append(w_spec(0))
    if swiglu:
        args.append(w)
        in_specs.append(w_spec(nj))
    coef = 1.0
    if resid is not None:
        res, g_slot, coef = resid
        args += [res, mod]
        in_specs += [pl.BlockSpec((tm, tn), lambda i, j: (i, j)), mod_spec(g_slot, tn, True)]

    body = functools.partial(_linear_kernel, has_norm=norm is not None, n_seq=n_seq,
                             swiglu=swiglu, has_resid=resid is not None, coef=coef)
    return pl.pallas_call(
        body,
        out_shape=jax.ShapeDtypeStruct((m_rows, n_out), out_dtype),
        grid=grid,
        in_specs=in_specs,
        out_specs=pl.BlockSpec((tm, tn), lambda i, j: (i, j)),
        scratch_shapes=scratch,
        compiler_params=_params("parallel", "arbitrary"),
        name="linear",
    )(*args)


def _rmsnorm_kernel(x_ref, g_ref, o_ref):
    x = x_ref[...]
    ms = jnp.mean(x * x, axis=-1, keepdims=True)
    o_ref[...] = x * lax.rsqrt(ms + EPS) * g_ref[...]


def _final_norm(x, gain):
    m_rows = x.shape[0]
    tm = min(512, m_rows)
    return pl.pallas_call(
        _rmsnorm_kernel,
        out_shape=jax.ShapeDtypeStruct(x.shape, F32),
        grid=(m_rows // tm,),
        in_specs=[pl.BlockSpec((tm, D_MODEL), lambda i: (i, 0)),
                  pl.BlockSpec((1, D_MODEL), lambda i: (0, 0))],
        out_specs=pl.BlockSpec((tm, D_MODEL), lambda i: (i, 0)),
        compiler_params=_params("parallel"),
        name="final_norm",
    )(x, gain.reshape(1, D_MODEL))


def _gla_kernel(*refs, chunk, sub, n_chunks, has_state):
    if has_state:
        (q_ref, k_ref, v_ref, g_ref, r_ref, wa2_ref, ba_ref, gain_ref, s0_ref,
         o_ref, sout_ref, s_ref) = refs
    else:
        (q_ref, k_ref, v_ref, g_ref, r_ref, wa2_ref, ba_ref, gain_ref,
         o_ref, sout_ref, s_ref) = refs
    t = pl.program_id(2)
    mx = BF16 if chunk >= 16 else F32

    @pl.when(t == 0)
    def _():
        if has_state:
            s_ref[...] = s0_ref[...]
        else:
            s_ref[...] = jnp.zeros_like(s_ref)

    row = lax.broadcasted_iota(jnp.int32, (chunk, chunk), 0)
    col = lax.broadcasted_iota(jnp.int32, (chunk, chunk), 1)
    tri = (col <= row).astype(mx)
    rowi = lax.broadcasted_iota(jnp.int32, (chunk, 1), 0)
    ones = jnp.ones((chunk, LANE), mx)

    for c in range(n_chunks):
        r0 = c * chunk
        q = q_ref[r0:r0 + chunk, :].astype(F32) * (GLA_DK ** -0.5)
        k = k_ref[r0:r0 + chunk, :].astype(F32)
        v = v_ref[r0:r0 + chunk, :].astype(mx)
        z = _mm(r_ref[r0:r0 + chunk, :].astype(mx), wa2_ref[...].astype(mx)) + ba_ref[...]
        la = (jnp.minimum(z, 0.0) - jnp.log1p(jnp.exp(-jnp.abs(z)))) * GLA_INV_TAU
        if mx == BF16:
            la_hi, la_lo = _split2(la)
            b = _mm(tri, la_hi) + _mm(tri, la_lo)
            bcol = _tn(la_hi, ones) + _tn(la_lo, ones)
        else:
            b = _mmh(tri, la)
            bcol = lax.dot_general(la, ones, (((0,), (0,)), ((), ())),
                                   preferred_element_type=F32, precision=HIGHEST)
        s_old = s_ref[...]
        o = _mm((q * jnp.exp(b)).astype(mx), s_old.astype(mx))
        b_last = b[chunk - 1:chunk, :]
        kb = (k * jnp.exp(b_last - b)).astype(mx)

        parts = []
        for band in range(chunk // sub):
            lo, hi = band * sub, (band + 1) * sub
            mid = lo + sub // 2
            bref = b[mid:mid + 1, :]
            qi = (q[lo:hi] * jnp.exp(b[lo:hi] - bref)).astype(mx)
            e = jnp.where(rowi < hi, bref - b, 0.0)
            ki = (k * jnp.exp(e)).astype(mx)
            parts.append(_nt(qi, ki))
        sc = parts[0] if len(parts) == 1 else jnp.concatenate(parts, axis=0)
        sc = jnp.where(col <= row, sc, 0.0)
        o = o + _mm(sc.astype(mx), v)

        s_ref[...] = jnp.exp(bcol[:, 0:1]) * s_old + _tn(kb, v)

        ms = jnp.mean(o * o, axis=-1, keepdims=True)
        y = o * lax.rsqrt(ms + EPS) * gain_ref[...]
        o_ref[r0:r0 + chunk, :] = (y * _silu(g_ref[r0:r0 + chunk, :].astype(F32))).astype(o_ref.dtype)

    @pl.when(t == pl.num_programs(2) - 1)
    def _():
        sout_ref[...] = s_ref[...]


def _gla_core(proj, r, wa2p, b_a, gain, s0, *, bsz, length):
    chunk = min(GLA_CHUNK, length)
    sub = min(GLA_SUB, chunk)
    tl = min(256, length)
    grid = (bsz, GLA_HEADS, length // tl)
    in_specs = [
        pl.BlockSpec((None, tl, GLA_DK), lambda b, h, t: (b, t, h)),
        pl.BlockSpec((None, tl, GLA_DK), lambda b, h, t: (b, t, GLA_HEADS + h)),
        pl.BlockSpec((None, tl, GLA_DV), lambda b, h, t: (b, t, GLA_HEADS + h)),
        pl.BlockSpec((None, tl, GLA_DV), lambda b, h, t: (b, t, 2 * GLA_HEADS + h)),
        pl.BlockSpec((None, tl, LANE), lambda b, h, t: (b, t, 0)),
        pl.BlockSpec((LANE, GLA_DK), lambda b, h, t: (0, h)),
        pl.BlockSpec((1, GLA_DK), lambda b, h, t: (0, h)),
        pl.BlockSpec((1, GLA_DV), lambda b, h, t: (0, 0)),
    ]
    args = [proj, proj, proj, proj, r, wa2p, b_a, gain]
    if s0 is not None:
        in_specs.append(pl.BlockSpec((None, None, GLA_DK, GLA_DV), lambda b, h, t: (b, h, 0, 0)))
        args.append(s0)
    body = functools.partial(_gla_kernel, chunk=chunk, sub=sub, n_chunks=tl // chunk,
                             has_state=s0 is not None)
    return pl.pallas_call(
        body,
        out_shape=(jax.ShapeDtypeStruct((bsz, length, GLA_VW), BF16),
                   jax.ShapeDtypeStruct((bsz, GLA_HEADS, GLA_DK, GLA_DV), F32)),
        grid=grid,
        in_specs=in_specs,
        out_specs=(pl.BlockSpec((None, tl, GLA_DV), lambda b, h, t: (b, t, h)),
                   pl.BlockSpec((None, None, GLA_DK, GLA_DV), lambda b, h, t: (b, h, 0, 0))),
        scratch_shapes=[pltpu.VMEM((GLA_DK, GLA_DV), F32)],
        compiler_params=_params("parallel", "parallel", "arbitrary"),
        name="gla_scan",
    )(*args)


def _diff_lambda(lam_ref, lambda_init):
    lv = lam_ref[...]
    a = jnp.sum(lv[0:1] * lv[1:2], axis=-1, keepdims=True)
    b = jnp.sum(lv[2:3] * lv[3:4], axis=-1, keepdims=True)
    return jnp.exp(a) - jnp.exp(b) + lambda_init


def _diff_finish(o, gain_ref, lambda_init):
    ms = jnp.mean(o * o, axis=-1, keepdims=True)
    return o * lax.rsqrt(ms + EPS) * gain_ref[...] * (1.0 - lambda_init)


def _diff_prompt_kernel(q_ref, k_ref, v_ref, lam_ref, gain_ref, o_ref, m_ref, l_ref, acc_ref,
                        *, tq, tk, lambda_init):
    qi = pl.program_id(2)
    kj = pl.program_id(3)

    @pl.when(kj == 0)
    def _():
        m_ref[...] = jnp.full_like(m_ref, -jnp.inf)
        l_ref[...] = jnp.zeros_like(l_ref)
        acc_ref[...] = jnp.zeros_like(acc_ref)

    def attend(diagonal):
        q = (q_ref[...] * (DIFF_DH ** -0.5)).astype(BF16)
        k = k_ref[...].astype(BF16)
        v = v_ref[...].astype(BF16)
        for m in range(2):
            sl = slice(m * DIFF_DH, (m + 1) * DIFF_DH)
            s = _nt(q[:, sl], k[:, sl])
            if diagonal:
                keep = (lax.broadcasted_iota(jnp.int32, (tq, tk), 1)
                        <= lax.broadcasted_iota(jnp.int32, (tq, tk), 0))
                s = jnp.where(keep, s, -jnp.inf)
            m_old = m_ref[m]
            m_new = jnp.maximum(m_old, jnp.max(s, axis=-1, keepdims=True))
            alpha = jnp.exp(m_old - m_new)
            p = jnp.exp(s - m_new)
            l_ref[m] = alpha * l_ref[m] + jnp.sum(p, axis=-1, keepdims=True)
            acc_ref[m] = alpha * acc_ref[m] + _mm(p.astype(BF16), v)
            m_ref[m] = m_new

    @pl.when(kj < qi)
    def _():
        attend(False)

    @pl.when(kj == qi)
    def _():
        attend(True)
        lam = _diff_lambda(lam_ref, lambda_init)
        o = acc_ref[0] / l_ref[0] - lam * (acc_ref[1] / l_ref[1])
        o_ref[...] = _diff_finish(o, gain_ref, lambda_init).astype(o_ref.dtype)


def _diff_prompt(proj, lam_vecs, gain, *, bsz, length, lambda_init):
    tq = tk = min(512, length)
    nq = length // tq
    hw = 2 * DIFF_DH
    body = functools.partial(_diff_prompt_kernel, tq=tq, tk=tk, lambda_init=lambda_init)
    return pl.pallas_call(
        body,
        out_shape=jax.ShapeDtypeStruct((bsz, length, DIFF_W), BF16),
        grid=(bsz, DIFF_HEADS, nq, nq),
        in_specs=[
            pl.BlockSpec((None, tq, hw), lambda b, h, i, j: (b, i, h)),
            pl.BlockSpec((None, tk, hw), lambda b, h, i, j: (b, jnp.minimum(j, i), DIFF_HEADS + h)),
            pl.BlockSpec((None, tk, hw),
                         lambda b, h, i, j: (b, jnp.minimum(j, i), 2 * DIFF_HEADS + h)),
            pl.BlockSpec((4, DIFF_DH), lambda b, h, i, j: (0, 0)),
            pl.BlockSpec((1, hw), lambda b, h, i, j: (0, 0)),
        ],
        out_specs=pl.BlockSpec((None, tq, hw), lambda b, h, i, j: (b, i, h)),
        scratch_shapes=[pltpu.VMEM((2, tq, 1), F32), pltpu.VMEM((2, tq, 1), F32),
                        pltpu.VMEM((2, tq, hw), F32)],
        compiler_params=_params("parallel", "parallel", "parallel", "arbitrary"),
        name="diff_attn_prompt",
    )(proj, proj, proj, lam_vecs, gain)


def _diff_decode_kernel(pt_ref, q_ref, *refs, n_steps, pps, n_new, lambda_init):
    del pt_ref
    kc_refs, vc_refs = refs[0:pps], refs[pps:2 * pps]
    (kn_ref, vn_ref, lam_ref, gain_ref, o_ref,
     qt_ref, bias_ref, m_ref, l_ref, acc_ref) = refs[2 * pps:]
    b = pl.program_id(0)
    j = pl.program_id(1)
    nrow = 2 * DIFF_HEADS * n_new
    hw = 2 * DIFF_DH
    page_cols = PAGE_SIZE * DIFF_HEADS

    def head_match(ncols):
        r = lax.broadcasted_iota(jnp.int32, (nrow, ncols), 0)
        c = lax.broadcasted_iota(jnp.int32, (nrow, ncols), 1)
        return r, c, (c % DIFF_HEADS) == ((r // n_new) % DIFF_HEADS)

    @pl.when((b == 0) & (j == 0))
    def _():
        _, _, same = head_match(page_cols)
        bias_ref[...] = jnp.where(same, 0.0, -jnp.inf)

    @pl.when(j == 0)
    def _():
        qt_ref[...] = jnp.zeros_like(qt_ref)
        for m in range(2):
            for h in range(DIFF_HEADS):
                r0 = (m * DIFF_HEADS + h) * n_new
                c0 = h * hw + m * DIFF_DH
                qt_ref[r0:r0 + n_new, m * DIFF_DH:(m + 1) * DIFF_DH] = (
                    q_ref[:, c0:c0 + DIFF_DH] * (DIFF_DH ** -0.5))
        m_ref[...] = jnp.full_like(m_ref, -jnp.inf)
        l_ref[...] = jnp.zeros_like(l_ref)
        acc_ref[...] = jnp.zeros_like(acc_ref)

    def attend(kvs, bias):
        qt = qt_ref[...].astype(BF16)
        scores = [_nt(qt, k2d.astype(BF16)) + bias for k2d, _ in kvs]
        m_old = m_ref[...]
        m_new = m_old
        for s in scores:
            m_new = jnp.maximum(m_new, jnp.max(s, axis=-1, keepdims=True))
        alpha = jnp.exp(m_old - m_new)
        l_new = alpha * l_ref[...]
        acc = alpha * acc_ref[...]
        for s, (_, v2d) in zip(scores, kvs):
            p = jnp.exp(s - m_new)
            l_new = l_new + jnp.sum(p, axis=-1, keepdims=True)
            acc = acc + _mm(p.astype(BF16), v2d.astype(BF16))
        l_ref[...] = l_new
        acc_ref[...] = acc
        m_ref[...] = m_new

    @pl.when(j < n_steps)
    def _():
        attend([(kr[...].reshape(page_cols, hw), vr[...].reshape(page_cols, hw))
                for kr, vr in zip(kc_refs, vc_refs)], bias_ref[...])

    @pl.when(j == n_steps)
    def _():
        r, c, same = head_match(n_new * DIFF_HEADS)
        keep = same & ((c // DIFF_HEADS) <= (r % n_new))
        attend([(kn_ref[...], vn_ref[...])], jnp.where(keep, 0.0, -jnp.inf))
        lam = _diff_lambda(lam_ref, lambda_init)
        o2 = acc_ref[...] / l_ref[...]
        half = nrow // 2
        o = o2[0:half] - lam * o2[half:nrow]
        y = _diff_finish(o, gain_ref, lambda_init)
        for h in range(DIFF_HEADS):
            o_ref[:, h * hw:(h + 1) * hw] = y[h * n_new:(h + 1) * n_new, :].astype(o_ref.dtype)


def _diff_decode(q, k_new, v_new, cache_k, cache_v, page_table, layer, lam_vecs, gain,
                 *, bsz, n_new, lambda_init):
    n_pages = page_table.shape[1]
    pps = 4 if n_pages % 4 == 0 else (2 if n_pages % 2 == 0 else 1)
    n_steps = n_pages // pps
    hw = 2 * DIFF_DH
    nrow = 2 * DIFF_HEADS * n_new
    page_blk = (None, None, PAGE_SIZE, DIFF_HEADS, hw)

    def page_spec(p):
        return pl.BlockSpec(
            page_blk,
            lambda b, j, pt: (layer, pt[b, jnp.minimum(j, n_steps - 1) * pps + p], 0, 0, 0))

    body = functools.partial(_diff_decode_kernel, n_steps=n_steps, pps=pps, n_new=n_new,
                             lambda_init=lambda_init)
    new_spec = pl.BlockSpec((None, n_new * DIFF_HEADS, hw), lambda b, j, pt: (b, 0, 0))
    grid_spec = pltpu.PrefetchScalarGridSpec(
        num_scalar_prefetch=1,
        grid=(bsz, n_steps + 1),
        in_specs=(
            [pl.BlockSpec((None, n_new, DIFF_W), lambda b, j, pt: (b, 0, 0))]
            + [page_spec(p) for p in range(pps)] * 2
            + [new_spec, new_spec,
               pl.BlockSpec((4, DIFF_DH), lambda b, j, pt: (0, 0)),
               pl.BlockSpec((1, hw), lambda b, j, pt: (0, 0))]),
        out_specs=pl.BlockSpec((None, n_new, DIFF_W), lambda b, j, pt: (b, 0, 0)),
        scratch_shapes=[
            pltpu.VMEM((nrow, hw), F32),
            pltpu.VMEM((nrow, PAGE_SIZE * DIFF_HEADS), F32),
            pltpu.VMEM((nrow, 1), F32),
            pltpu.VMEM((nrow, 1), F32),
            pltpu.VMEM((nrow, hw), F32),
        ],
    )
    return pl.pallas_call(
        body,
        out_shape=jax.ShapeDtypeStruct((bsz, n_new, DIFF_W), BF16),
        grid_spec=grid_spec,
        compiler_params=_params("arbitrary", "arbitrary"),
        name="diff_attn_decode",
    )(page_table, q, *([cache_k] * pps), *([cache_v] * pps), k_new, v_new, lam_vecs, gain)


def _gdn_prep_kernel(x_ref, buf_ref, w_ref, o_ref, nb_ref, *, length, tc, nq_blocks, nqk_blocks):
    j = pl.program_id(1)
    x = x_ref[...].astype(F32)
    buf = buf_ref[...]
    w = w_ref[...]
    row = lax.broadcasted_iota(jnp.int32, (length, 1), 0)
    b0, b1, b2 = buf[0:1], buf[1:2], buf[2:3]
    x1 = jnp.where(row >= 1, pltpu.roll(x, 1, 0), b2)
    x2 = jnp.where(row >= 2, pltpu.roll(x, 2, 0), jnp.where(row == 1, b2, b1))
    x3 = jnp.where(row >= 3, pltpu.roll(x, 3, 0),
                   jnp.where(row == 2, b2, jnp.where(row == 1, b1, b0)))
    y = w[3:4] * x + w[2:3] * x1 + w[1:2] * x2 + w[0:1] * x3
    y = _silu(y)
    nb_ref[...] = x[length - (GDN_CONV - 1):length, :]

    @pl.when(j < nqk_blocks)
    def _():
        scale = jnp.where(j < nq_blocks, GDN_DK ** -0.5, 1.0)
        for h in range(tc // GDN_DK):
            sl = slice(h * GDN_DK, (h + 1) * GDN_DK)
            yh = y[:, sl]
            ss = jnp.sum(yh * yh, axis=-1, keepdims=True)
            o_ref[:, sl] = (yh * (lax.rsqrt(ss + EPS) * scale)).astype(o_ref.dtype)

    @pl.when(j >= nqk_blocks)
    def _():
        o_ref[...] = y.astype(o_ref.dtype)


def _gdn_prep(proj, buf, conv_w, *, bsz, length):
    tc = 512
    body = functools.partial(_gdn_prep_kernel, length=length, tc=tc,
                             nq_blocks=GDN_KW // tc, nqk_blocks=2 * GDN_KW // tc)
    return pl.pallas_call(
        body,
        out_shape=(jax.ShapeDtypeStruct((bsz, length, GDN_QKV_W), BF16),
                   jax.ShapeDtypeStruct((bsz, GDN_CONV - 1, GDN_QKV_W), F32)),
        grid=(bsz, GDN_QKV_W // tc),
        in_specs=[
            pl.BlockSpec((None, length, tc), lambda b, j: (b, 0, j)),
            pl.BlockSpec((None, GDN_CONV - 1, tc), lambda b, j: (b, 0, j)),
            pl.BlockSpec((GDN_CONV, tc), lambda b, j: (0, j)),
        ],
        out_specs=(pl.BlockSpec((None, length, tc), lambda b, j: (b, 0, j)),
                   pl.BlockSpec((None, GDN_CONV - 1, tc), lambda b, j: (b, 0, j))),
        compiler_params=_params("parallel", "parallel"),
        name="gdn_conv_prep",
    )(proj, buf, conv_w)


def _gdn_kernel(*refs, chunk, has_state):
    if has_state:
        (q_ref, k_ref, v_ref, z_ref, bg_ref, alog_ref, dtb_ref, gain_ref, s0_ref,
         o_ref, sout_ref, s_ref) = refs
    else:
        (q_ref, k_ref, v_ref, z_ref, bg_ref, alog_ref, dtb_ref, gain_ref,
         o_ref, sout_ref, s_ref) = refs
    t = pl.program_id(2)
    big = chunk >= 16
    mx = BF16 if big else F32
    rep = GDN_GV // GDN_GQ

    @pl.when(t == 0)
    def _():
        if has_state:
            s_ref[...] = s0_ref[...]
        else:
            s_ref[...] = jnp.zeros_like(s_ref)

    row = lax.broadcasted_iota(jnp.int32, (chunk, chunk), 0)
    col = lax.broadcasted_iota(jnp.int32, (chunk, chunk), 1)
    lower = col <= row
    strict = col < row
    eye_f = (col == row).astype(F32)

    def bmm(a, b):
        return lax.dot_general(a, b, (((2,), (1,)), ((0,), (0,))), preferred_element_type=F32)

    def bmm_f32(a, b):
        if not big:
            return lax.dot_general(a, b, (((2,), (1,)), ((0,), (0,))),
                                   preferred_element_type=F32, precision=HIGHEST)
        a_hi, a_lo = _split2(a)
        b_hi, b_lo = _split2(b)
        return bmm(a_hi, b_hi) + (bmm(a_lo, b_hi) + bmm(a_hi, b_lo))

    def bnt(a, b):
        return lax.dot_general(a, b, (((2,), (2,)), ((0,), (0,))), preferred_element_type=F32)

    bg = bg_ref[...]
    beta_all = jax.nn.sigmoid(bg[:, 0:GDN_GV])
    a_in = bg[:, GDN_GV:2 * GDN_GV] + dtb_ref[...]
    softplus = jnp.maximum(a_in, 0.0) + jnp.log1p(jnp.exp(-jnp.abs(a_in)))
    g_all = -jnp.exp(alog_ref[...]) * softplus
    if big:
        g1 = g_all.astype(BF16)
        r1 = g_all - g1.astype(F32)
        g2 = r1.astype(BF16)
        g3 = (r1 - g2.astype(F32)).astype(BF16)
        low_b = lower.astype(BF16)
        up_b = (row <= col).astype(BF16)
        gc_all = _mm(low_b, g1) + (_mm(low_b, g2) + _mm(low_b, g3))
        gr_all = _tn(g1, up_b) + (_tn(g2, up_b) + _tn(g3, up_b))
    else:
        gc_all = _mmh(lower.astype(F32), g_all)
        gr_all = lax.dot_general(g_all, (row <= col).astype(F32), (((0,), (0,)), ((), ())),
                                 preferred_element_type=F32, precision=HIGHEST)

    heads = range(GDN_GV)
    gcol = jnp.stack([gc_all[:, h:h + 1] for h in heads])
    grow = jnp.stack([gr_all[h:h + 1, :] for h in heads])
    beta = jnp.stack([beta_all[:, h:h + 1] for h in heads])
    q4 = jnp.stack([q_ref[:, h * GDN_DK:(h + 1) * GDN_DK].astype(F32) for h in range(GDN_GQ)])
    k4 = jnp.stack([k_ref[:, h * GDN_DK:(h + 1) * GDN_DK].astype(F32) for h in range(GDN_GQ)])
    kk4 = bnt(k4.astype(mx), k4.astype(mx))
    qk4 = bnt(q4.astype(mx), k4.astype(mx))
    kk = jnp.stack([kk4[h // rep] for h in heads])
    qk = jnp.stack([qk4[h // rep] for h in heads])
    k8 = jnp.stack([k4[h // rep] for h in heads])
    v8 = jnp.stack([v_ref[:, h * GDN_DV:(h + 1) * GDN_DV].astype(F32) for h in heads])

    decay = jnp.exp(jnp.where(lower[None], gcol - grow, -jnp.inf))
    m = jnp.where(strict[None], beta * kk * decay, 0.0)

    def sub_block(size):
        return (((row // (2 * size)) == (col // (2 * size)))
                & ((row // size) % 2 == 1) & ((col // size) % 2 == 0))[None]

    inv = eye_f[None] - jnp.where(sub_block(1), m, 0.0)
    size = 2
    while size < chunk:
        low = jnp.where(sub_block(size), m, 0.0)
        inv = inv - bmm_f32(bmm_f32(inv, low), inv)
        size *= 2

    eg = jnp.exp(gcol)
    rhs = jnp.concatenate([beta * v8, (beta * eg) * k8], axis=-1)
    sol = bmm_f32(inv, rhs)
    attn = jnp.where(lower[None], qk * decay, 0.0).astype(mx)
    g_last = gcol[:, chunk - 1:chunk, :]
    kd = (k8 * jnp.exp(g_last - gcol)).astype(mx)
    qe = (jnp.stack([q4[h // rep] for h in heads]) * eg).astype(mx)
    s_scale = jnp.exp(g_last)

    for h in heads:
        vs = slice(h * GDN_DV, (h + 1) * GDN_DV)
        s_old = s_ref[h]
        s_mx = s_old.astype(mx)
        delta = sol[h, :, 0:GDN_DV] - _mm(sol[h, :, GDN_DV:].astype(mx), s_mx)
        o = _mm(qe[h], s_mx) + _mm(attn[h], delta.astype(mx))
        s_ref[h] = s_scale[h] * s_old + _tn(kd[h], delta.astype(mx))
        ms = jnp.mean(o * o, axis=-1, keepdims=True)
        y = o * lax.rsqrt(ms + EPS) * gain_ref[...]
        o_ref[:, vs] = (y * _silu(z_ref[:, vs].astype(F32))).astype(o_ref.dtype)

    @pl.when(t == pl.num_programs(2) - 1)
    def _():
        sout_ref[...] = s_ref[...]


def _gdn_core(qkv, proj, bg, a_log, dt_bias, gain, s0, *, bsz, length):
    chunk = min(GDN_CHUNK, length)
    qw = GDN_GQ * GDN_DK
    vw = GDN_GV * GDN_DV
    in_specs = [
        pl.BlockSpec((None, chunk, qw), lambda b, g, t: (b, t, g)),
        pl.BlockSpec((None, chunk, qw), lambda b, g, t: (b, t, GDN_KW // qw + g)),
        pl.BlockSpec((None, chunk, vw), lambda b, g, t: (b, t, 2 * GDN_KW // vw + g)),
        pl.BlockSpec((None, chunk, vw), lambda b, g, t: (b, t, GDN_QKV_W // vw + g)),
        pl.BlockSpec((None, None, chunk, 2 * GDN_GV), lambda b, g, t: (b, g, t, 0)),
        pl.BlockSpec((None, 1, GDN_GV), lambda b, g, t: (g, 0, 0)),
        pl.BlockSpec((None, 1, GDN_GV), lambda b, g, t: (g, 0, 0)),
        pl.BlockSpec((1, GDN_DV), lambda b, g, t: (0, 0)),
    ]
    args = [qkv, qkv, qkv, proj, bg, a_log, dt_bias, gain]
    if s0 is not None:
        in_specs.append(pl.BlockSpec((None, GDN_GV, GDN_DK, GDN_DV), lambda b, g, t: (b, g, 0, 0)))
        args.append(s0)
    body = functools.partial(_gdn_kernel, chunk=chunk, has_state=s0 is not None)
    return pl.pallas_call(
        body,
        out_shape=(jax.ShapeDtypeStruct((bsz, length, GDN_VW), BF16),
                   jax.ShapeDtypeStruct((bsz, GDN_V_HEADS, GDN_DK, GDN_DV), F32)),
        grid=(bsz, GDN_GROUPS, length // chunk),
        in_specs=in_specs,
        out_specs=(pl.BlockSpec((None, chunk, vw), lambda b, g, t: (b, t, g)),
                   pl.BlockSpec((None, GDN_GV, GDN_DK, GDN_DV), lambda b, g, t: (b, g, 0, 0))),
        scratch_shapes=[pltpu.VMEM((GDN_GV, GDN_DK, GDN_DV), F32)],
        compiler_params=_params("parallel", "parallel", "arbitrary"),
        name="gdn_scan",
    )(*args)


def _trunk(x, seq0, w, past, mod):
    bsz, length, _ = x.shape
    m_rows = bsz * length
    x = x.reshape(m_rows, D_MODEL)
    seq = (seq0, length)
    lin = functools.partial(_linear, seq=seq, mod=mod, tm=1024)
    new_gla, new_dk, new_dv, new_gs, new_gc = [], [], [], [], []

    def ffn(x, i, which, slots):
        sh, sc, g = slots
        gain = w['norm_gain'][i, 2 * which].reshape(1, D_MODEL)
        act = lin(x, w['ffn_w_up'], (i, which), n_out=D_FF, tn=512, out_dtype=BF16, layer=i,
                  norm=(gain, sh, sc), swiglu=True)
        return lin(act, w['ffn_w_down'], (i, which), n_out=D_MODEL, tn=256, out_dtype=F32,
                   layer=i, resid=(x, g, 0.5))

    for i in range(DEPTH):
        kind, j = i % 3, i // 3
        x = ffn(x, i, 0, (0, 1, 2))
        gain2 = w['norm_gain'][i, 1].reshape(1, D_MODEL)
        norm2 = (gain2, 3, 4)
        if kind == 0:
            proj = lin(x, w['gla_w_in'], (j,), n_out=2 * GLA_KW + 2 * GLA_VW, tn=512,
                       out_dtype=BF16, layer=i, norm=norm2)
            wa1p = jnp.pad(w['gla_w_a1'][j], ((0, 0), (0, LANE - GLA_RANK)))
            r = lin(x, wa1p, (), n_out=LANE, tn=LANE, out_dtype=F32, layer=i, norm=norm2)
            wa2p = jnp.pad(w['gla_w_a2'][j], ((0, LANE - GLA_RANK), (0, 0)))
            s0 = None if past is None else past['state_gla'][j]
            o, s = _gla_core(proj.reshape(bsz, length, -1), r.reshape(bsz, length, LANE), wa2p,
                             w['gla_b_a'][j].reshape(1, GLA_KW), w['gla_norm'][j].reshape(1, GLA_DV),
                             s0, bsz=bsz, length=length)
            new_gla.append(s)
            w_o, w_o_pre = w['gla_w_o'], (j,)
        elif kind == 1:
            lambda_init = 0.8 - 0.6 * math.exp(-0.3 * i)
            proj = lin(x, w['diff_w_in'], (j,), n_out=3 * DIFF_W, tn=512, out_dtype=F32,
                       layer=i, norm=norm2)
            proj3 = proj.reshape(bsz, length, 3 * DIFF_W)
            k_new = proj3[:, :, DIFF_W:2 * DIFF_W].reshape(bsz, length, DIFF_HEADS, 2 * DIFF_DH)
            v_new = proj3[:, :, 2 * DIFF_W:].reshape(bsz, length, DIFF_HEADS, 2 * DIFF_DH)
            gain = w['diff_norm'][j].reshape(1, 2 * DIFF_DH)
            if past is None:
                o = _diff_prompt(proj3, w['diff_lambda'][j], gain, bsz=bsz, length=length,
                                 lambda_init=lambda_init)
            else:
                o = _diff_decode(proj3[:, :, 0:DIFF_W],
                                 k_new.reshape(bsz, length * DIFF_HEADS, 2 * DIFF_DH),
                                 v_new.reshape(bsz, length * DIFF_HEADS, 2 * DIFF_DH),
                                 past['cache_diff_k'], past['cache_diff_v'], past['page_table'], j,
                                 w['diff_lambda'][j], gain, bsz=bsz, n_new=length,
                                 lambda_init=lambda_init)
            new_dk.append(k_new)
            new_dv.append(v_new)
            w_o, w_o_pre = w['diff_w_o'], (j,)
        else:
            main_w = GDN_QKV_W + GDN_VW
            proj = lin(x, w['gdn_w_in'], (j,), n_out=main_w, tn=512, out_dtype=BF16, layer=i,
                       norm=norm2)
            w_ba = jnp.pad(w['gdn_w_in'][j][:, main_w:], ((0, 0), (0, LANE - 2 * GDN_V_HEADS)))
            ba = lin(x, w_ba, (), n_out=LANE, tn=LANE, out_dtype=F32, layer=i, norm=norm2)
            ba = ba.reshape(bsz, length, LANE)[:, :, 0:2 * GDN_V_HEADS]
            bg = ba.reshape(bsz, length, 2, GDN_GROUPS, GDN_GV)
            bg = jnp.transpose(bg, (0, 3, 1, 2, 4)).reshape(bsz, GDN_GROUPS, length, 2 * GDN_GV)
            proj3 = proj.reshape(bsz, length, main_w)
            if past is None:
                buf = jnp.zeros((bsz, GDN_CONV - 1, GDN_QKV_W), F32)
                s0 = None
            else:
                buf = past['state_gdn_conv'][j]
                s0 = past['state_gdn'][j]
            qkv, nbuf = _gdn_prep(proj3, buf, w['gdn_conv_w'][j], bsz=bsz, length=length)
            o, s = _gdn_core(qkv, proj3, bg,
                             w['gdn_a_log'][j].reshape(GDN_GROUPS, 1, GDN_GV),
                             w['gdn_dt_bias'][j].reshape(GDN_GROUPS, 1, GDN_GV),
                             w['gdn_norm'][j].reshape(1, GDN_DV), s0, bsz=bsz, length=length)
            new_gs.append(s)
            new_gc.append(nbuf)
            w_o, w_o_pre = w['gdn_w_o'], (j,)
        o2 = o.reshape(m_rows, -1)
        x = lin(o2, w_o, w_o_pre, n_out=D_MODEL, tn=512, out_dtype=F32, layer=i,
                resid=(x, 5, 1.0))
        x = ffn(x, i, 1, (6, 7, 8))
    y = _final_norm(x, w['final_gain']).reshape(bsz, length, D_MODEL)
    return y, (new_gla, new_dk, new_dv, new_gs, new_gc)


def kernel(x_prompt, x_sample, cache_diff_k, cache_diff_v, state_gla, state_gdn, state_gdn_conv,
           page_table, c_prompt, c_sample, ada_w, ada_b, norm_gain, final_gain, ffn_w_up,
           ffn_w_down, gla_w_in, gla_w_a1, gla_w_a2, gla_b_a, gla_norm, gla_w_o, diff_w_in,
           diff_lambda, diff_norm, diff_w_o, gdn_w_in, gdn_conv_w, gdn_a_log, gdn_dt_bias,
           gdn_norm, gdn_w_o):
    w = {
        'norm_gain': norm_gain, 'final_gain': final_gain,
        'ffn_w_up': ffn_w_up, 'ffn_w_down': ffn_w_down,
        'gla_w_in': gla_w_in, 'gla_w_a1': gla_w_a1, 'gla_w_a2': gla_w_a2, 'gla_b_a': gla_b_a,
        'gla_norm': gla_norm, 'gla_w_o': gla_w_o,
        'diff_w_in': diff_w_in, 'diff_lambda': diff_lambda, 'diff_norm': diff_norm,
        'diff_w_o': diff_w_o,
        'gdn_w_in': gdn_w_in, 'gdn_conv_w': gdn_conv_w, 'gdn_a_log': gdn_a_log,
        'gdn_dt_bias': gdn_dt_bias, 'gdn_norm': gdn_norm, 'gdn_w_o': gdn_w_o,
    }
    n_s, n_p = c_sample.shape[0], c_prompt.shape[0]
    n_rows = 16
    c_all = jnp.concatenate(
        [c_sample, c_prompt, jnp.zeros((n_rows - n_s - n_p, D_MODEL), F32)], axis=0)
    mod = _ada(c_all, ada_w, ada_b).reshape(DEPTH, n_rows, 1, N_MOD * D_MODEL)

    y_p, (gla_p, dk_p, dv_p, gdn_p, conv_p) = _trunk(x_prompt, n_s, w, None, mod)
    past = {
        'state_gla': state_gla, 'cache_diff_k': cache_diff_k, 'cache_diff_v': cache_diff_v,
        'page_table': page_table, 'state_gdn': state_gdn, 'state_gdn_conv': state_gdn_conv,
    }
    y_s, (gla_s, dk_s, dv_s, gdn_s, conv_s) = _trunk(x_sample, 0, w, past, mod)
    return (y_p, y_s,
            jnp.stack(gla_p), jnp.stack(gla_s),
            jnp.stack(dk_p), jnp.stack(dv_p), jnp.stack(dk_s), jnp.stack(dv_s),
            jnp.stack(gdn_p), jnp.stack(gdn_s),
            jnp.stack(conv_p), jnp.stack(conv_s))
```

```python
import functools
import math

import jax
import jax.numpy as jnp
from jax import lax
from jax.experimental import pallas as pl
from jax.experimental.pallas import tpu as pltpu

F32 = jnp.float32
BF16 = jnp.bfloat16
HIGHEST = lax.Precision.HIGHEST

D_MODEL = 2048
DEPTH = 4
EPS = 1e-6
N_MOD = 9
D_FF = 5632

GLA_HEADS = 4
GLA_DK = 256
GLA_DV = 512
GLA_KW = GLA_HEADS * GLA_DK
GLA_VW = GLA_HEADS * GLA_DV
GLA_RANK = 16
GLA_INV_TAU = 1.0 / 16.0
GLA_CHUNK = 64
GLA_SUB = 16

DIFF_DH = 128
DIFF_HEADS = 8
DIFF_W = 2048
PAGE_SIZE = 128

GDN_DK = 128
GDN_DV = 128
GDN_QK_HEADS = 16
GDN_V_HEADS = 32
GDN_KW = 2048
GDN_VW = 4096
GDN_QKV_W = 8192
GDN_CONV = 4
GDN_CHUNK = 64
GDN_GROUPS = 4
GDN_GQ = GDN_QK_HEADS // GDN_GROUPS
GDN_GV = GDN_V_HEADS // GDN_GROUPS

LANE = 128
VMEM_LIMIT = 56 * 1024 * 1024


def _params(*sem):
    return pltpu.CompilerParams(dimension_semantics=sem, vmem_limit_bytes=VMEM_LIMIT)


def _nt(a, b):
    return lax.dot_general(a, b, (((1,), (1,)), ((), ())), preferred_element_type=F32)


def _tn(a, b):
    return lax.dot_general(a, b, (((0,), (0,)), ((), ())), preferred_element_type=F32)


def _mm(a, b):
    return jnp.dot(a, b, preferred_element_type=F32)


def _mmh(a, b):
    return jnp.dot(a, b, preferred_element_type=F32, precision=HIGHEST)


def _silu(x):
    return x * jax.nn.sigmoid(x)


def _lane_tile(x, n):
    return x if n == 1 else jnp.concatenate([x] * n, axis=1)


def _split2(x):
    hi = x.astype(BF16)
    lo = (x - hi.astype(F32)).astype(BF16)
    return hi, lo


def _ada_kernel(c_ref, w_ref, b_ref, o_ref):
    s = _silu(c_ref[...]).astype(BF16)
    o_ref[...] = _mm(s, w_ref[...].astype(BF16)) + b_ref[...]


def _ada(c_all, ada_w, ada_b):
    rows = c_all.shape[0]
    width = ada_w.shape[2]
    tn = 1024
    return pl.pallas_call(
        _ada_kernel,
        out_shape=jax.ShapeDtypeStruct((DEPTH, rows, width), F32),
        grid=(DEPTH, width // tn),
        in_specs=[
            pl.BlockSpec((rows, D_MODEL), lambda l, j: (0, 0)),
            pl.BlockSpec((None, D_MODEL, tn), lambda l, j: (l, 0, j)),
            pl.BlockSpec((None, 1, tn), lambda l, j: (l, 0, j)),
        ],
        out_specs=pl.BlockSpec((None, rows, tn), lambda l, j: (l, 0, j)),
        compiler_params=_params("parallel", "parallel"),
        name="ada_mod",
    )(c_all, ada_w, ada_b.reshape(DEPTH, 1, width))


def _linear_kernel(*refs, has_norm, n_seq, swiglu, has_resid, coef, n_split, cols_per_split):
    it = iter(refs)
    x_refs, sh_refs, sc_refs = [], [], []
    for _ in range(2):
        x_refs.append(next(it))
        if has_norm:
            sh_refs.append(next(it))
            sc_refs.append(next(it))
    gain_ref = next(it) if has_norm else None
    w_ref = next(it)
    wu_ref = next(it) if swiglu else None
    res_refs, gate_refs = [], []
    if has_resid:
        for _ in range(2):
            res_refs.append(next(it))
            gate_refs.append(next(it))
    o_refs = [[next(it) for _ in range(n_split)] for _ in range(2)]
    h_refs = [next(it), next(it)] if has_norm else None
    i = pl.program_id(0)
    j = pl.program_id(1)

    def run(g):
        def per_seq(val, vec_ref, fn):
            if n_seq[g] == 1:
                return fn(val, vec_ref[0])
            rows, n = val.shape
            return fn(val.reshape(n_seq[g], rows // n_seq[g], n), vec_ref[...]).reshape(rows, n)

        if has_norm:
            @pl.when(j == 0)
            def _():
                xf = x_refs[g][...].astype(F32)
                ms = jnp.mean(xf * xf, axis=-1, keepdims=True)
                y = xf * lax.rsqrt(ms + EPS) * gain_ref[...]
                y = per_seq(y, sc_refs[g], lambda a, s: a * (1.0 + s))
                y = per_seq(y, sh_refs[g], lambda a, s: a + s)
                h_refs[g][...] = y.astype(BF16)
            h = h_refs[g][...]
        else:
            h = x_refs[g][...].astype(BF16)

        acc = _mm(h, w_ref[...].astype(BF16))
        if swiglu:
            up = _mm(h, wu_ref[...].astype(BF16))
            acc = _silu(acc) * up
        if has_resid:
            acc = per_seq(acc, gate_refs[g], lambda a, gt: a * (coef * gt))
            acc = res_refs[g][...] + acc
        if n_split == 1:
            o_refs[g][0][...] = acc.astype(o_refs[g][0].dtype)
        else:
            for s in range(n_split):
                @pl.when(j // cols_per_split == s)
                def _():
                    o_refs[g][s][...] = acc.astype(o_refs[g][s].dtype)

    run(0)

    @pl.when(i == pl.num_programs(0) - 1)
    def _():
        run(1)


def _linear(xs, w, w_prefix, *, n_out, tm, tn, out_dtype, seqs, mod, layer,
            norm=None, swiglu=False, resid=None, n_split=1):
    m0, k_dim = xs[0].shape
    m1 = xs[1].shape[0]
    tm = min(tm, m0)
    n_i = m0 // tm
    nj = n_out // tn
    cols_per_split = nj // n_split
    seq0_a, len_a = seqs[0]
    seq0_b, len_b = seqs[1]
    n_seq = (max(1, tm // len_a), m1 // len_b)
    tiles_per_seq = max(1, len_a // tm)
    npre = len(w_prefix)

    def seq_blk(g, i):
        if g == 1:
            return seq0_b // n_seq[1]
        if n_seq[0] == 1:
            return seq0_a + i // tiles_per_seq
        return seq0_a // n_seq[0] + i

    def col(g, i, c):
        return c if g == 0 else jnp.where(i == n_i - 1, c, 0)

    def mod_spec(g, slot, width, per_j):
        blocks = D_MODEL // width
        if per_j:
            return pl.BlockSpec((None, n_seq[g], 1, width),
                                lambda i, j: (layer, seq_blk(g, i), 0, slot * blocks + col(g, i, j)))
        return pl.BlockSpec((None, n_seq[g], 1, width),
                            lambda i, j: (layer, seq_blk(g, i), 0, slot * blocks))

    def row_spec(g, width, per_j):
        rows = tm if g == 0 else m1
        if per_j:
            return pl.BlockSpec((rows, width), lambda i, j: (i if g == 0 else 0, col(g, i, j)))
        return pl.BlockSpec((rows, width), lambda i, j: (i if g == 0 else 0, 0))

    def out_spec(g, s):
        rows = tm if g == 0 else m1
        return pl.BlockSpec(
            (rows, tn),
            lambda i, j: (i if g == 0 else 0,
                          col(g, i, jnp.clip(j - s * cols_per_split, 0, cols_per_split - 1))))

    def w_spec(col0):
        return pl.BlockSpec((None,) * npre + (k_dim, tn),
                            lambda i, j: tuple(w_prefix) + (0, col0 + j))

    args, in_specs, scratch = [], [], []
    for g in range(2):
        args.append(xs[g])
        in_specs.append(row_spec(g, k_dim, False))
        if norm is not None:
            args += [mod, mod]
            in_specs += [mod_spec(g, norm[1], D_MODEL, False), mod_spec(g, norm[2], D_MODEL, False)]
    if norm is not None:
        args.append(norm[0])
        in_specs.append(pl.BlockSpec((1, k_dim), lambda i, j: (0, 0)))
        scratch += [pltpu.VMEM((tm, k_dim), BF16), pltpu.VMEM((m1, k_dim), BF16)]
    args.append(w)
    in_specs.append(w_spec(0))
    if swiglu:
        args.append(w)
        in_specs.append(w_spec(nj))
    coef = 1.0
    if resid is not None:
        res, g_slot, coef = resid
        for g in range(2):
            args += [res[g], mod]
            in_specs += [row_spec(g, tn, True), mod_spec(g, g_slot, tn, True)]

    body = functools.partial(_linear_kernel, has_norm=norm is not None, n_seq=n_seq,
                             swiglu=swiglu, has_resid=resid is not None, coef=coef,
                             n_split=n_split, cols_per_split=cols_per_split)
    width = n_out // n_split
    outs = pl.pallas_call(
        body,
        out_shape=[jax.ShapeDtypeStruct((m, width), out_dtype)
                   for m in (m0, m1) for _ in range(n_split)],
        grid=(n_i, nj),
        in_specs=in_specs,
        out_specs=[out_spec(g, s) for g in range(2) for s in range(n_split)],
        scratch_shapes=scratch,
        compiler_params=_params("arbitrary", "arbitrary"),
        name="linear",
    )(*args)
    if n_split == 1:
        return [outs[0], outs[1]]
    return [outs[0:n_split], outs[n_split:2 * n_split]]


def _rmsnorm_kernel(x_ref, g_ref, o_ref):
    x = x_ref[...]
    ms = jnp.mean(x * x, axis=-1, keepdims=True)
    o_ref[...] = x * lax.rsqrt(ms + EPS) * g_ref[...]


def _final_norm(x, gain):
    m_rows = x.shape[0]
    tm = min(512, m_rows)
    return pl.pallas_call(
        _rmsnorm_kernel,
        out_shape=jax.ShapeDtypeStruct(x.shape, F32),
        grid=(m_rows // tm,),
        in_specs=[pl.BlockSpec((tm, D_MODEL), lambda i: (i, 0)),
                  pl.BlockSpec((1, D_MODEL), lambda i: (0, 0))],
        out_specs=pl.BlockSpec((tm, D_MODEL), lambda i: (i, 0)),
        compiler_params=_params("parallel"),
        name="final_norm",
    )(x, gain.reshape(1, D_MODEL))


def _gla_kernel(*refs, chunk, sub, n_chunks, has_state):
    if has_state:
        (q_ref, k_ref, v_ref, g_ref, r_ref, wa2_ref, ba_ref, gain_ref, s0_ref,
         o_ref, sout_ref, s_ref) = refs
    else:
        (q_ref, k_ref, v_ref, g_ref, r_ref, wa2_ref, ba_ref, gain_ref,
         o_ref, sout_ref, s_ref) = refs
    t = pl.program_id(2)
    mx = BF16 if chunk >= 16 else F32

    @pl.when(t == 0)
    def _():
        if has_state:
            s_ref[...] = s0_ref[...]
        else:
            s_ref[...] = jnp.zeros_like(s_ref)

    row = lax.broadcasted_iota(jnp.int32, (chunk, chunk), 0)
    col = lax.broadcasted_iota(jnp.int32, (chunk, chunk), 1)
    tri = (col <= row).astype(mx)
    rowi = lax.broadcasted_iota(jnp.int32, (chunk, 1), 0)
    ones = jnp.ones((chunk, LANE), mx)

    for c in range(n_chunks):
        r0 = c * chunk
        q = q_ref[r0:r0 + chunk, :].astype(F32) * (GLA_DK ** -0.5)
        k = k_ref[r0:r0 + chunk, :].astype(F32)
        v = v_ref[r0:r0 + chunk, :].astype(mx)
        z = _mm(r_ref[r0:r0 + chunk, :].astype(mx), wa2_ref[...].astype(mx)) + ba_ref[...]
        la = (jnp.minimum(z, 0.0) - jnp.log1p(jnp.exp(-jnp.abs(z)))) * GLA_INV_TAU
        if mx == BF16:
            la_hi, la_lo = _split2(la)
            b = _mm(tri, la_hi) + _mm(tri, la_lo)
            bcol = _tn(la_hi, ones) + _tn(la_lo, ones)
        else:
            b = _mmh(tri, la)
            bcol = lax.dot_general(la, ones, (((0,), (0,)), ((), ())),
                                   preferred_element_type=F32, precision=HIGHEST)
        s_old = s_ref[...]
        o = _mm((q * jnp.exp(b)).astype(mx), s_old.astype(mx))
        b_last = b[chunk - 1:chunk, :]
        kb = (k * jnp.exp(b_last - b)).astype(mx)

        parts = []
        for band in range(chunk // sub):
            lo, hi = band * sub, (band + 1) * sub
            mid = lo + sub // 2
            bref = b[mid:mid + 1, :]
            qi = (q[lo:hi] * jnp.exp(b[lo:hi] - bref)).astype(mx)
            e = jnp.where(rowi < hi, bref - b, 0.0)
            ki = (k * jnp.exp(e)).astype(mx)
            parts.append(_nt(qi, ki))
        sc = parts[0] if len(parts) == 1 else jnp.concatenate(parts, axis=0)
        sc = jnp.where(col <= row, sc, 0.0)
        o = o + _mm(sc.astype(mx), v)

        s_ref[...] = jnp.exp(bcol[:, 0:1]) * s_old + _tn(kb, v)

        ms = jnp.mean(o * o, axis=-1, keepdims=True)
        y = o * lax.rsqrt(ms + EPS) * gain_ref[...]
        o_ref[r0:r0 + chunk, :] = (y * _silu(g_ref[r0:r0 + chunk, :].astype(F32))).astype(o_ref.dtype)

    @pl.when(t == pl.num_programs(2) - 1)
    def _():
        sout_ref[...] = s_ref[...]


def _gla_core(proj, r, wa2p, b_a, gain, s0, *, bsz, length):
    chunk = min(GLA_CHUNK, length)
    sub = min(GLA_SUB, chunk)
    tl = min(256, length)
    grid = (bsz, GLA_HEADS, length // tl)
    in_specs = [
        pl.BlockSpec((None, tl, GLA_DK), lambda b, h, t: (b, t, h)),
        pl.BlockSpec((None, tl, GLA_DK), lambda b, h, t: (b, t, GLA_HEADS + h)),
        pl.BlockSpec((None, tl, GLA_DV), lambda b, h, t: (b, t, GLA_HEADS + h)),
        pl.BlockSpec((None, tl, GLA_DV), lambda b, h, t: (b, t, 2 * GLA_HEADS + h)),
        pl.BlockSpec((None, tl, LANE), lambda b, h, t: (b, t, 0)),
        pl.BlockSpec((LANE, GLA_DK), lambda b, h, t: (0, h)),
        pl.BlockSpec((1, GLA_DK), lambda b, h, t: (0, h)),
        pl.BlockSpec((1, GLA_DV), lambda b, h, t: (0, 0)),
    ]
    args = [proj, proj, proj, proj, r, wa2p, b_a, gain]
    if s0 is not None:
        in_specs.append(pl.BlockSpec((None, None, GLA_DK, GLA_DV), lambda b, h, t: (b, h, 0, 0)))
        args.append(s0)
    body = functools.partial(_gla_kernel, chunk=chunk, sub=sub, n_chunks=tl // chunk,
                             has_state=s0 is not None)
    return pl.pallas_call(
        body,
        out_shape=(jax.ShapeDtypeStruct((bsz, length, GLA_VW), BF16),
                   jax.ShapeDtypeStruct((bsz, GLA_HEADS, GLA_DK, GLA_DV), F32)),
        grid=grid,
        in_specs=in_specs,
        out_specs=(pl.BlockSpec((None, tl, GLA_DV), lambda b, h, t: (b, t, h)),
                   pl.BlockSpec((None, None, GLA_DK, GLA_DV), lambda b, h, t: (b, h, 0, 0))),
        scratch_shapes=[pltpu.VMEM((GLA_DK, GLA_DV), F32)],
        compiler_params=_params("parallel", "parallel", "arbitrary"),
        name="gla_scan",
    )(*args)


def _diff_lambda(lam_ref, lambda_init):
    lv = lam_ref[...]
    a = jnp.sum(lv[0:1] * lv[1:2], axis=-1, keepdims=True)
    b = jnp.sum(lv[2:3] * lv[3:4], axis=-1, keepdims=True)
    return jnp.exp(a) - jnp.exp(b) + lambda_init


def _diff_finish(o, gain_ref, lambda_init):
    ms = jnp.mean(o * o, axis=-1, keepdims=True)
    return o * lax.rsqrt(ms + EPS) * gain_ref[...] * (1.0 - lambda_init)


def _diff_prompt_kernel(q_ref, k_ref, v_ref, lam_ref, gain_ref, o_ref, m_ref, l_ref, acc_ref,
                        *, tq, tk, lambda_init):
    qi = pl.program_id(2)
    kj = pl.program_id(3)

    @pl.when(kj == 0)
    def _():
        m_ref[...] = jnp.full_like(m_ref, -jnp.inf)
        l_ref[...] = jnp.zeros_like(l_ref)
        acc_ref[...] = jnp.zeros_like(acc_ref)

    def attend(diagonal):
        q = (q_ref[...] * (DIFF_DH ** -0.5)).astype(BF16)
        k = k_ref[...].astype(BF16)
        v = v_ref[...].astype(BF16)
        for m in range(2):
            sl = slice(m * DIFF_DH, (m + 1) * DIFF_DH)
            s = _nt(q[:, sl], k[:, sl])
            if diagonal:
                keep = (lax.broadcasted_iota(jnp.int32, (tq, tk), 1)
                        <= lax.broadcasted_iota(jnp.int32, (tq, tk), 0))
                s = jnp.where(keep, s, -jnp.inf)
            m_old = m_ref[m]
            m_new = jnp.maximum(m_old, jnp.max(s, axis=-1, keepdims=True))
            alpha = jnp.exp(m_old - m_new)
            p = jnp.exp(s - _lane_tile(m_new, tk // LANE))
            l_ref[m] = alpha * l_ref[m] + jnp.sum(p, axis=-1, keepdims=True)
            acc_ref[m] = _lane_tile(alpha, 2 * DIFF_DH // LANE) * acc_ref[m] + _mm(p.astype(BF16), v)
            m_ref[m] = m_new

    @pl.when(kj < qi)
    def _():
        attend(False)

    @pl.when(kj == qi)
    def _():
        attend(True)
        lam = _diff_lambda(lam_ref, lambda_init)
        reps = 2 * DIFF_DH // LANE
        o = (acc_ref[0] / _lane_tile(l_ref[0], reps)
             - lam * (acc_ref[1] / _lane_tile(l_ref[1], reps)))
        o_ref[...] = _diff_finish(o, gain_ref, lambda_init).astype(o_ref.dtype)


def _diff_prompt(q, k, v, lam_vecs, gain, *, bsz, length, lambda_init):
    tq = tk = min(512, length)
    nq = length // tq
    hw = 2 * DIFF_DH
    body = functools.partial(_diff_prompt_kernel, tq=tq, tk=tk, lambda_init=lambda_init)
    return pl.pallas_call(
        body,
        out_shape=jax.ShapeDtypeStruct((bsz, length, DIFF_W), BF16),
        grid=(bsz, DIFF_HEADS, nq, nq),
        in_specs=[
            pl.BlockSpec((None, tq, hw), lambda b, h, i, j: (b, i, h)),
            pl.BlockSpec((None, tk, hw), lambda b, h, i, j: (b, jnp.minimum(j, i), h)),
            pl.BlockSpec((None, tk, hw), lambda b, h, i, j: (b, jnp.minimum(j, i), h)),
            pl.BlockSpec((4, DIFF_DH), lambda b, h, i, j: (0, 0)),
            pl.BlockSpec((1, hw), lambda b, h, i, j: (0, 0)),
        ],
        out_specs=pl.BlockSpec((None, tq, hw), lambda b, h, i, j: (b, i, h)),
        scratch_shapes=[pltpu.VMEM((2, tq, LANE), F32), pltpu.VMEM((2, tq, LANE), F32),
                        pltpu.VMEM((2, tq, hw), F32)],
        compiler_params=_params("parallel", "parallel", "parallel", "arbitrary"),
        name="diff_attn_prompt",
    )(q, k, v, lam_vecs, gain)


def _diff_decode_kernel(pt_ref, q_ref, *refs, n_steps, pps, n_new, lambda_init):
    del pt_ref
    kc_refs, vc_refs = refs[0:pps], refs[pps:2 * pps]
    (kn_ref, vn_ref, lam_ref, gain_ref, o_ref,
     qt_ref, bias_ref, m_ref, l_ref, acc_ref) = refs[2 * pps:]
    b = pl.program_id(0)
    j = pl.program_id(1)
    nrow = 2 * DIFF_HEADS * n_new
    hw = 2 * DIFF_DH
    page_cols = PAGE_SIZE * DIFF_HEADS

    def head_match(ncols):
        r = lax.broadcasted_iota(jnp.int32, (nrow, ncols), 0)
        c = lax.broadcasted_iota(jnp.int32, (nrow, ncols), 1)
        return r, c, (c % DIFF_HEADS) == ((r // n_new) % DIFF_HEADS)

    @pl.when((b == 0) & (j == 0))
    def _():
        _, _, same = head_match(page_cols)
        bias_ref[...] = jnp.where(same, 0.0, -jnp.inf)

    @pl.when(j == 0)
    def _():
        qt_ref[...] = jnp.zeros_like(qt_ref)
        for m in range(2):
            for h in range(DIFF_HEADS):
                r0 = (m * DIFF_HEADS + h) * n_new
                c0 = h * hw + m * DIFF_DH
                qt_ref[r0:r0 + n_new, m * DIFF_DH:(m + 1) * DIFF_DH] = (
                    q_ref[:, c0:c0 + DIFF_DH] * (DIFF_DH ** -0.5))
        m_ref[...] = jnp.full_like(m_ref, -jnp.inf)
        l_ref[...] = jnp.zeros_like(l_ref)
        acc_ref[...] = jnp.zeros_like(acc_ref)

    def attend(kvs, bias):
        qt = qt_ref[...].astype(BF16)
        scores = [_nt(qt, k2d.astype(BF16)) + bias for k2d, _ in kvs]
        ncols = bias.shape[1]
        m_old = m_ref[...]
        m_new = m_old
        for s in scores:
            m_new = jnp.maximum(m_new, jnp.max(s, axis=-1, keepdims=True))
        alpha = jnp.exp(m_old - m_new)
        l_new = alpha * l_ref[...]
        acc = _lane_tile(alpha, hw // LANE) * acc_ref[...]
        m_wide = m_new[:, 0:ncols] if ncols <= LANE else _lane_tile(m_new, ncols // LANE)
        for s, (_, v2d) in zip(scores, kvs):
            p = jnp.exp(s - m_wide)
            l_new = l_new + jnp.sum(p, axis=-1, keepdims=True)
            acc = acc + _mm(p.astype(BF16), v2d.astype(BF16))
        l_ref[...] = l_new
        acc_ref[...] = acc
        m_ref[...] = m_new

    @pl.when(j < n_steps)
    def _():
        attend([(kr[...].reshape(page_cols, hw), vr[...].reshape(page_cols, hw))
                for kr, vr in zip(kc_refs, vc_refs)], bias_ref[...])

    @pl.when(j == n_steps)
    def _():
        r, c, same = head_match(n_new * DIFF_HEADS)
        keep = same & ((c // DIFF_HEADS) <= (r % n_new))
        attend([(kn_ref[...], vn_ref[...])], jnp.where(keep, 0.0, -jnp.inf))
        lam = _diff_lambda(lam_ref, lambda_init)
        o2 = acc_ref[...] / _lane_tile(l_ref[...], hw // LANE)
        half = nrow // 2
        o = o2[0:half] - lam * o2[half:nrow]
        y = _diff_finish(o, gain_ref, lambda_init)
        for h in range(DIFF_HEADS):
            o_ref[:, h * hw:(h + 1) * hw] = y[h * n_new:(h + 1) * n_new, :].astype(o_ref.dtype)


def _diff_decode(q, k_new, v_new, cache_k, cache_v, page_table, layer, lam_vecs, gain,
                 *, bsz, n_new, lambda_init):
    n_pages = page_table.shape[1]
    pps = 4 if n_pages % 4 == 0 else (2 if n_pages % 2 == 0 else 1)
    n_steps = n_pages // pps
    hw = 2 * DIFF_DH
    nrow = 2 * DIFF_HEADS * n_new
    page_blk = (None, None, PAGE_SIZE, DIFF_HEADS, hw)

    def page_spec(p):
        return pl.BlockSpec(
            page_blk,
            lambda b, j, pt: (layer, pt[b, jnp.minimum(j, n_steps - 1) * pps + p], 0, 0, 0))

    body = functools.partial(_diff_decode_kernel, n_steps=n_steps, pps=pps, n_new=n_new,
                             lambda_init=lambda_init)
    new_spec = pl.BlockSpec((None, n_new * DIFF_HEADS, hw), lambda b, j, pt: (b, 0, 0))
    grid_spec = pltpu.PrefetchScalarGridSpec(
        num_scalar_prefetch=1,
        grid=(bsz, n_steps + 1),
        in_specs=(
            [pl.BlockSpec((None, n_new, DIFF_W), lambda b, j, pt: (b, 0, 0))]
            + [page_spec(p) for p in range(pps)] * 2
            + [new_spec, new_spec,
               pl.BlockSpec((4, DIFF_DH), lambda b, j, pt: (0, 0)),
               pl.BlockSpec((1, hw), lambda b, j, pt: (0, 0))]),
        out_specs=pl.BlockSpec((None, n_new, DIFF_W), lambda b, j, pt: (b, 0, 0)),
        scratch_shapes=[
            pltpu.VMEM((nrow, hw), F32),
            pltpu.VMEM((nrow, PAGE_SIZE * DIFF_HEADS), F32),
            pltpu.VMEM((nrow, LANE), F32),
            pltpu.VMEM((nrow, LANE), F32),
            pltpu.VMEM((nrow, hw), F32),
        ],
    )
    return pl.pallas_call(
        body,
        out_shape=jax.ShapeDtypeStruct((bsz, n_new, DIFF_W), BF16),
        grid_spec=grid_spec,
        compiler_params=_params("arbitrary", "arbitrary"),
        name="diff_attn_decode",
    )(page_table, q, *([cache_k] * pps), *([cache_v] * pps), k_new, v_new, lam_vecs, gain)


def _gdn_prep_kernel(x_ref, buf_ref, w_ref, o_ref, nb_ref, *, length, tc, nq_blocks, nqk_blocks):
    j = pl.program_id(1)
    x = x_ref[...].astype(F32)
    buf = buf_ref[...]
    w = w_ref[...]
    row = lax.broadcasted_iota(jnp.int32, (length, 1), 0)
    b0, b1, b2 = buf[0:1], buf[1:2], buf[2:3]
    x1 = jnp.where(row >= 1, pltpu.roll(x, 1, 0), b2)
    x2 = jnp.where(row >= 2, pltpu.roll(x, 2, 0), jnp.where(row == 1, b2, b1))
    x3 = jnp.where(row >= 3, pltpu.roll(x, 3, 0),
                   jnp.where(row == 2, b2, jnp.where(row == 1, b1, b0)))
    y = w[3:4] * x + w[2:3] * x1 + w[1:2] * x2 + w[0:1] * x3
    y = _silu(y)
    nb_ref[...] = x[length - (GDN_CONV - 1):length, :]

    @pl.when(j < nqk_blocks)
    def _():
        scale = jnp.where(j < nq_blocks, GDN_DK ** -0.5, 1.0)
        for h in range(tc // GDN_DK):
            sl = slice(h * GDN_DK, (h + 1) * GDN_DK)
            yh = y[:, sl]
            ss = jnp.sum(yh * yh, axis=-1, keepdims=True)
            o_ref[:, sl] = (yh * (lax.rsqrt(ss + EPS) * scale)).astype(o_ref.dtype)

    @pl.when(j >= nqk_blocks)
    def _():
        o_ref[...] = y.astype(o_ref.dtype)


def _gdn_prep(proj, buf, conv_w, *, bsz, length):
    tc = 512
    body = functools.partial(_gdn_prep_kernel, length=length, tc=tc,
                             nq_blocks=GDN_KW // tc, nqk_blocks=2 * GDN_KW // tc)
    return pl.pallas_call(
        body,
        out_shape=(jax.ShapeDtypeStruct((bsz, length, GDN_QKV_W), BF16),
                   jax.ShapeDtypeStruct((bsz, GDN_CONV - 1, GDN_QKV_W), F32)),
        grid=(bsz, GDN_QKV_W // tc),
        in_specs=[
            pl.BlockSpec((None, length, tc), lambda b, j: (b, 0, j)),
            pl.BlockSpec((None, GDN_CONV - 1, tc), lambda b, j: (b, 0, j)),
            pl.BlockSpec((GDN_CONV, tc), lambda b, j: (0, j)),
        ],
        out_specs=(pl.BlockSpec((None, length, tc), lambda b, j: (b, 0, j)),
                   pl.BlockSpec((None, GDN_CONV - 1, tc), lambda b, j: (b, 0, j))),
        compiler_params=_params("parallel", "parallel"),
        name="gdn_conv_prep",
    )(proj, buf, conv_w)


def _gdn_kernel(*refs, chunk, n_chunks, has_state):
    if has_state:
        (q_ref, k_ref, v_ref, z_ref, bg_ref, alog_ref, dtb_ref, gain_ref, s0_ref,
         o_ref, sout_ref, s_ref) = refs
    else:
        (q_ref, k_ref, v_ref, z_ref, bg_ref, alog_ref, dtb_ref, gain_ref,
         o_ref, sout_ref, s_ref) = refs
    t = pl.program_id(2)
    big = chunk >= 16
    mx = BF16 if big else F32
    rep = GDN_GV // GDN_GQ

    @pl.when(t == 0)
    def _():
        if has_state:
            s_ref[...] = s0_ref[...]
        else:
            s_ref[...] = jnp.zeros_like(s_ref)

    row = lax.broadcasted_iota(jnp.int32, (chunk, chunk), 0)
    col = lax.broadcasted_iota(jnp.int32, (chunk, chunk), 1)
    lower = col <= row
    strict = col < row
    eye_f = (col == row).astype(F32)

    def bmm(a, b):
        return lax.dot_general(a, b, (((2,), (1,)), ((0,), (0,))), preferred_element_type=F32)

    def bmm_f32(a, b):
        if not big:
            return lax.dot_general(a, b, (((2,), (1,)), ((0,), (0,))),
                                   preferred_element_type=F32, precision=HIGHEST)
        a_hi, a_lo = _split2(a)
        b_hi, b_lo = _split2(b)
        return bmm(a_hi, b_hi) + (bmm(a_lo, b_hi) + bmm(a_hi, b_lo))

    def bnt(a, b):
        return lax.dot_general(a, b, (((2,), (2,)), ((0,), (0,))), preferred_element_type=F32)

    heads = range(GDN_GV)

    def sub_block(size):
        return (((row // (2 * size)) == (col // (2 * size)))
                & ((row // size) % 2 == 1) & ((col // size) % 2 == 0))[None]

    def state_free(r0):
        rows = slice(r0, r0 + chunk)
        bg = bg_ref[rows, :]
        beta_all = jax.nn.sigmoid(bg[:, 0:GDN_GV])
        a_in = bg[:, GDN_GV:2 * GDN_GV] + dtb_ref[...]
        softplus = jnp.maximum(a_in, 0.0) + jnp.log1p(jnp.exp(-jnp.abs(a_in)))
        g_all = -jnp.exp(alog_ref[...]) * softplus
        if big:
            g1 = g_all.astype(BF16)
            r1 = g_all - g1.astype(F32)
            g2 = r1.astype(BF16)
            g3 = (r1 - g2.astype(F32)).astype(BF16)
            low_b = lower.astype(BF16)
            up_b = (row <= col).astype(BF16)
            gc_all = _mm(low_b, g1) + (_mm(low_b, g2) + _mm(low_b, g3))
            gr_all = _tn(g1, up_b) + (_tn(g2, up_b) + _tn(g3, up_b))
        else:
            gc_all = _mmh(lower.astype(F32), g_all)
            gr_all = lax.dot_general(g_all, (row <= col).astype(F32), (((0,), (0,)), ((), ())),
                                     preferred_element_type=F32, precision=HIGHEST)

        gcol = jnp.stack([gc_all[:, h:h + 1] for h in heads])
        grow = jnp.stack([gr_all[h:h + 1, :] for h in heads])
        beta = jnp.stack([beta_all[:, h:h + 1] for h in heads])
        q4 = jnp.stack([q_ref[rows, h * GDN_DK:(h + 1) * GDN_DK].astype(F32)
                        for h in range(GDN_GQ)])
        k4 = jnp.stack([k_ref[rows, h * GDN_DK:(h + 1) * GDN_DK].astype(F32)
                        for h in range(GDN_GQ)])
        kk4 = bnt(k4.astype(mx), k4.astype(mx))
        qk4 = bnt(q4.astype(mx), k4.astype(mx))
        kk = jnp.stack([kk4[h // rep] for h in heads])
        qk = jnp.stack([qk4[h // rep] for h in heads])
        k8 = jnp.stack([k4[h // rep] for h in heads])
        v8 = jnp.stack([v_ref[rows, h * GDN_DV:(h + 1) * GDN_DV].astype(F32) for h in heads])

        decay = jnp.exp(jnp.where(lower[None], gcol - grow, -jnp.inf))
        m = jnp.where(strict[None], beta * kk * decay, 0.0)

        inv = eye_f[None] - jnp.where(sub_block(1), m, 0.0)
        size = 2
        while size < chunk:
            low = jnp.where(sub_block(size), m, 0.0)
            inv = inv - bmm_f32(bmm_f32(inv, low), inv)
            size *= 2

        eg = jnp.exp(gcol)
        rhs = jnp.concatenate([beta * v8, (beta * eg) * k8], axis=-1)
        sol = bmm_f32(inv, rhs)
        attn = jnp.where(lower[None], qk * decay, 0.0).astype(mx)
        g_last = gcol[:, chunk - 1:chunk, :]
        kd = (k8 * jnp.exp(g_last - gcol)).astype(mx)
        qe = (jnp.stack([q4[h // rep] for h in heads]) * eg).astype(mx)
        return sol, attn, kd, qe, jnp.exp(g_last)

    def recur(r0, parts):
        sol, attn, kd, qe, s_scale = parts
        rows = slice(r0, r0 + chunk)
        for h in heads:
            vs = slice(h * GDN_DV, (h + 1) * GDN_DV)
            s_old = s_ref[h]
            s_mx = s_old.astype(mx)
            delta = sol[h, :, 0:GDN_DV] - _mm(sol[h, :, GDN_DV:].astype(mx), s_mx)
            o = _mm(qe[h], s_mx) + _mm(attn[h], delta.astype(mx))
            s_ref[h] = s_scale[h] * s_old + _tn(kd[h], delta.astype(mx))
            ms = jnp.mean(o * o, axis=-1, keepdims=True)
            y = o * lax.rsqrt(ms + EPS) * gain_ref[...]
            o_ref[rows, vs] = (y * _silu(z_ref[rows, vs].astype(F32))).astype(o_ref.dtype)

    parts = [state_free(c * chunk) for c in range(n_chunks)]
    for c in range(n_chunks):
        recur(c * chunk, parts[c])

    @pl.when(t == pl.num_programs(2) - 1)
    def _():
        sout_ref[...] = s_ref[...]


def _gdn_core(qkv, proj, bg, a_log, dt_bias, gain, s0, *, bsz, length):
    chunk = min(GDN_CHUNK, length)
    tl = min(2 * GDN_CHUNK, length)
    qw = GDN_GQ * GDN_DK
    vw = GDN_GV * GDN_DV
    in_specs = [
        pl.BlockSpec((None, tl, qw), lambda b, g, t: (b, t, g)),
        pl.BlockSpec((None, tl, qw), lambda b, g, t: (b, t, GDN_KW // qw + g)),
        pl.BlockSpec((None, tl, vw), lambda b, g, t: (b, t, 2 * GDN_KW // vw + g)),
        pl.BlockSpec((None, tl, vw), lambda b, g, t: (b, t, GDN_QKV_W // vw + g)),
        pl.BlockSpec((None, None, tl, 2 * GDN_GV), lambda b, g, t: (b, g, t, 0)),
        pl.BlockSpec((None, 1, GDN_GV), lambda b, g, t: (g, 0, 0)),
        pl.BlockSpec((None, 1, GDN_GV), lambda b, g, t: (g, 0, 0)),
        pl.BlockSpec((1, GDN_DV), lambda b, g, t: (0, 0)),
    ]
    args = [qkv, qkv, qkv, proj, bg, a_log, dt_bias, gain]
    if s0 is not None:
        in_specs.append(pl.BlockSpec((None, GDN_GV, GDN_DK, GDN_DV), lambda b, g, t: (b, g, 0, 0)))
        args.append(s0)
    body = functools.partial(_gdn_kernel, chunk=chunk, n_chunks=tl // chunk,
                             has_state=s0 is not None)
    return pl.pallas_call(
        body,
        out_shape=(jax.ShapeDtypeStruct((bsz, length, GDN_VW), BF16),
                   jax.ShapeDtypeStruct((bsz, GDN_V_HEADS, GDN_DK, GDN_DV), F32)),
        grid=(bsz, GDN_GROUPS, length // tl),
        in_specs=in_specs,
        out_specs=(pl.BlockSpec((None, tl, vw), lambda b, g, t: (b, t, g)),
                   pl.BlockSpec((None, GDN_GV, GDN_DK, GDN_DV), lambda b, g, t: (b, g, 0, 0))),
        scratch_shapes=[pltpu.VMEM((GDN_GV, GDN_DK, GDN_DV), F32)],
        compiler_params=_params("parallel", "parallel", "arbitrary"),
        name="gdn_scan",
    )(*args)


def _trunk(x_prompt, x_sample, w, past, mod, n_s):
    dims = [x_prompt.shape[0:2], x_sample.shape[0:2]]
    x = [x_prompt.reshape(-1, D_MODEL), x_sample.reshape(-1, D_MODEL)]
    seqs = [(n_s, dims[0][1]), (0, dims[1][1])]
    lin = functools.partial(_linear, seqs=seqs, mod=mod, tm=1024)
    groups = range(2)
    new_gla, new_dk, new_dv, new_gs, new_gc = ([[], []] for _ in range(5))

    def ffn(x, i, which, slots):
        sh, sc, g = slots
        gain = w['norm_gain'][i, 2 * which].reshape(1, D_MODEL)
        act = lin(x, w['ffn_w_up'], (i, which), n_out=D_FF, tn=512, out_dtype=BF16, layer=i,
                  norm=(gain, sh, sc), swiglu=True)
        return lin(act, w['ffn_w_down'], (i, which), n_out=D_MODEL, tn=256, out_dtype=F32,
                   layer=i, resid=(x, g, 0.5))

    for i in range(DEPTH):
        kind, j = i % 3, i // 3
        x = ffn(x, i, 0, (0, 1, 2))
        gain2 = w['norm_gain'][i, 1].reshape(1, D_MODEL)
        norm2 = (gain2, 3, 4)
        o = []
        if kind == 0:
            proj = lin(x, w['gla_w_in'], (j,), n_out=2 * GLA_KW + 2 * GLA_VW, tn=512,
                       out_dtype=BF16, layer=i, norm=norm2)
            wa1p = jnp.pad(w['gla_w_a1'][j], ((0, 0), (0, LANE - GLA_RANK)))
            r = lin(x, wa1p, (), n_out=LANE, tn=LANE, out_dtype=F32, layer=i, norm=norm2)
            wa2p = jnp.pad(w['gla_w_a2'][j], ((0, LANE - GLA_RANK), (0, 0)))
            for g in groups:
                bsz, length = dims[g]
                s0 = None if g == 0 else past['state_gla'][j]
                og, s = _gla_core(proj[g].reshape(bsz, length, -1), r[g].reshape(bsz, length, LANE),
                                  wa2p, w['gla_b_a'][j].reshape(1, GLA_KW),
                                  w['gla_norm'][j].reshape(1, GLA_DV), s0, bsz=bsz, length=length)
                o.append(og)
                new_gla[g].append(s)
            w_o, w_o_pre = w['gla_w_o'], (j,)
        elif kind == 1:
            lambda_init = 0.8 - 0.6 * math.exp(-0.3 * i)
            qkv = lin(x, w['diff_w_in'], (j,), n_out=3 * DIFF_W, tn=512, out_dtype=F32,
                      layer=i, norm=norm2, n_split=3)
            gain = w['diff_norm'][j].reshape(1, 2 * DIFF_DH)
            for g in groups:
                bsz, length = dims[g]
                q, k, v = (t.reshape(bsz, length, DIFF_W) for t in qkv[g])
                if g == 0:
                    og = _diff_prompt(q, k, v, w['diff_lambda'][j], gain, bsz=bsz, length=length,
                                      lambda_init=lambda_init)
                else:
                    og = _diff_decode(q, k.reshape(bsz, length * DIFF_HEADS, 2 * DIFF_DH),
                                      v.reshape(bsz, length * DIFF_HEADS, 2 * DIFF_DH),
                                      past['cache_diff_k'], past['cache_diff_v'],
                                      past['page_table'], j, w['diff_lambda'][j], gain, bsz=bsz,
                                      n_new=length, lambda_init=lambda_init)
                o.append(og)
                new_dk[g].append(k.reshape(bsz, length, DIFF_HEADS, 2 * DIFF_DH))
                new_dv[g].append(v.reshape(bsz, length, DIFF_HEADS, 2 * DIFF_DH))
            w_o, w_o_pre = w['diff_w_o'], (j,)
        else:
            main_w = GDN_QKV_W + GDN_VW
            proj = lin(x, w['gdn_w_in'], (j,), n_out=main_w, tn=512, out_dtype=BF16, layer=i,
                       norm=norm2)
            w_ba = jnp.pad(w['gdn_w_in'][j][:, main_w:], ((0, 0), (0, LANE - 2 * GDN_V_HEADS)))
            ba = lin(x, w_ba, (), n_out=LANE, tn=LANE, out_dtype=F32, layer=i, norm=norm2)
            for g in groups:
                bsz, length = dims[g]
                bg = ba[g].reshape(bsz, length, LANE)[:, :, 0:2 * GDN_V_HEADS]
                bg = bg.reshape(bsz, length, 2, GDN_GROUPS, GDN_GV)
                bg = jnp.transpose(bg, (0, 3, 1, 2, 4)).reshape(bsz, GDN_GROUPS, length, 2 * GDN_GV)
                proj3 = proj[g].reshape(bsz, length, main_w)
                if g == 0:
                    buf = jnp.zeros((bsz, GDN_CONV - 1, GDN_QKV_W), F32)
                    s0 = None
                else:
                    buf = past['state_gdn_conv'][j]
                    s0 = past['state_gdn'][j]
                qkv, nbuf = _gdn_prep(proj3, buf, w['gdn_conv_w'][j], bsz=bsz, length=length)
                og, s = _gdn_core(qkv, proj3, bg,
                                  w['gdn_a_log'][j].reshape(GDN_GROUPS, 1, GDN_GV),
                                  w['gdn_dt_bias'][j].reshape(GDN_GROUPS, 1, GDN_GV),
                                  w['gdn_norm'][j].reshape(1, GDN_DV), s0, bsz=bsz, length=length)
                o.append(og)
                new_gs[g].append(s)
                new_gc[g].append(nbuf)
            w_o, w_o_pre = w['gdn_w_o'], (j,)
        o = [o[g].reshape(x[g].shape[0], -1) for g in groups]
        x = lin(o, w_o, w_o_pre, n_out=D_MODEL, tn=512, out_dtype=F32, layer=i,
                resid=(x, 5, 1.0))
        x = ffn(x, i, 1, (6, 7, 8))
    y = [_final_norm(x[g], w['final_gain']).reshape(*dims[g], D_MODEL) for g in groups]
    return y, new_gla, new_dk, new_dv, new_gs, new_gc


def kernel(x_prompt, x_sample, cache_diff_k, cache_diff_v, state_gla, state_gdn, state_gdn_conv,
           page_table, c_prompt, c_sample, ada_w, ada_b, norm_gain, final_gain, ffn_w_up,
           ffn_w_down, gla_w_in, gla_w_a1, gla_w_a2, gla_b_a, gla_norm, gla_w_o, diff_w_in,
           diff_lambda, diff_norm, diff_w_o, gdn_w_in, gdn_conv_w, gdn_a_log, gdn_dt_bias,
           gdn_norm, gdn_w_o):
    w = {
        'norm_gain': norm_gain, 'final_gain': final_gain,
        'ffn_w_up': ffn_w_up, 'ffn_w_down': ffn_w_down,
        'gla_w_in': gla_w_in, 'gla_w_a1': gla_w_a1, 'gla_w_a2': gla_w_a2, 'gla_b_a': gla_b_a,
        'gla_norm': gla_norm, 'gla_w_o': gla_w_o,
        'diff_w_in': diff_w_in, 'diff_lambda': diff_lambda, 'diff_norm': diff_norm,
        'diff_w_o': diff_w_o,
        'gdn_w_in': gdn_w_in, 'gdn_conv_w': gdn_conv_w, 'gdn_a_log': gdn_a_log,
        'gdn_dt_bias': gdn_dt_bias, 'gdn_norm': gdn_norm, 'gdn_w_o': gdn_w_o,
    }
    n_s, n_p = c_sample.shape[0], c_prompt.shape[0]
    n_rows = 16
    c_all = jnp.concatenate(
        [c_sample, c_prompt, jnp.zeros((n_rows - n_s - n_p, D_MODEL), F32)], axis=0)
    mod = _ada(c_all, ada_w, ada_b).reshape(DEPTH, n_rows, 1, N_MOD * D_MODEL)

    past = {
        'state_gla': state_gla, 'cache_diff_k': cache_diff_k, 'cache_diff_v': cache_diff_v,
        'page_table': page_table, 'state_gdn': state_gdn, 'state_gdn_conv': state_gdn_conv,
    }
    y, gla, dk, dv, gdn, conv = _trunk(x_prompt, x_sample, w, past, mod, n_s)
    return (y[0], y[1],
            jnp.stack(gla[0]), jnp.stack(gla[1]),
            jnp.stack(dk[0]), jnp.stack(dv[0]), jnp.stack(dk[1]), jnp.stack(dv[1]),
            jnp.stack(gdn[0]), jnp.stack(gdn[1]),
            jnp.stack(conv[0]), jnp.stack(conv[1]))
```

```python
import functools
import math

import jax
import jax.numpy as jnp
from jax import lax
from jax.experimental import pallas as pl
from jax.experimental.pallas import tpu as pltpu

F32 = jnp.float32
BF16 = jnp.bfloat16
HIGHEST = lax.Precision.HIGHEST

D_MODEL = 2048
DEPTH = 4
EPS = 1e-6
N_MOD = 9
D_FF = 5632

GLA_HEADS = 4
GLA_DK = 256
GLA_DV = 512
GLA_KW = GLA_HEADS * GLA_DK
GLA_VW = GLA_HEADS * GLA_DV
GLA_RANK = 16
GLA_INV_TAU = 1.0 / 16.0
GLA_CHUNK = 64
GLA_SUB = 16

DIFF_DH = 128
DIFF_HEADS = 8
DIFF_W = 2048
PAGE_SIZE = 128

GDN_DK = 128
GDN_DV = 128
GDN_QK_HEADS = 16
GDN_V_HEADS = 32
GDN_KW = 2048
GDN_VW = 4096
GDN_QKV_W = 8192
GDN_CONV = 4
GDN_CHUNK = 64
GDN_GROUPS = 4
GDN_GQ = GDN_QK_HEADS // GDN_GROUPS
GDN_GV = GDN_V_HEADS // GDN_GROUPS

LANE = 128
VMEM_LIMIT = 56 * 1024 * 1024


def _params(*sem):
    return pltpu.CompilerParams(dimension_semantics=sem, vmem_limit_bytes=VMEM_LIMIT)


def _nt(a, b):
    return lax.dot_general(a, b, (((1,), (1,)), ((), ())), preferred_element_type=F32)


def _tn(a, b):
    return lax.dot_general(a, b, (((0,), (0,)), ((), ())), preferred_element_type=F32)


def _mm(a, b):
    return jnp.dot(a, b, preferred_element_type=F32)


def _mmh(a, b):
    return jnp.dot(a, b, preferred_element_type=F32, precision=HIGHEST)


def _silu(x):
    return x * jax.nn.sigmoid(x)


def _lane_tile(x, n):
    return x if n == 1 else jnp.concatenate([x] * n, axis=1)


def _split2(x):
    hi = x.astype(BF16)
    lo = (x - hi.astype(F32)).astype(BF16)
    return hi, lo


def _ada_kernel(c_ref, w_ref, b_ref, o_ref):
    s = _silu(c_ref[...]).astype(BF16)
    o_ref[...] = _mm(s, w_ref[...].astype(BF16)) + b_ref[...]


def _ada(c_all, ada_w, ada_b):
    rows = c_all.shape[0]
    width = ada_w.shape[2]
    tn = 1024
    return pl.pallas_call(
        _ada_kernel,
        out_shape=jax.ShapeDtypeStruct((DEPTH, rows, width), F32),
        grid=(DEPTH, width // tn),
        in_specs=[
            pl.BlockSpec((rows, D_MODEL), lambda l, j: (0, 0)),
            pl.BlockSpec((None, D_MODEL, tn), lambda l, j: (l, 0, j)),
            pl.BlockSpec((None, 1, tn), lambda l, j: (l, 0, j)),
        ],
        out_specs=pl.BlockSpec((None, rows, tn), lambda l, j: (l, 0, j)),
        compiler_params=_params("parallel", "parallel"),
        name="ada_mod",
    )(c_all, ada_w, ada_b.reshape(DEPTH, 1, width))


def _linear_kernel(*refs, has_norm, n_seq, swiglu, has_resid, coef, n_split, cols_per_split):
    it = iter(refs)
    x_refs, sh_refs, sc_refs = [], [], []
    for _ in range(2):
        x_refs.append(next(it))
        if has_norm:
            sh_refs.append(next(it))
            sc_refs.append(next(it))
    gain_ref = next(it) if has_norm else None
    w_ref = next(it)
    wu_ref = next(it) if swiglu else None
    res_refs, gate_refs = [], []
    if has_resid:
        for _ in range(2):
            res_refs.append(next(it))
            gate_refs.append(next(it))
    o_refs = [[next(it) for _ in range(n_split)] for _ in range(2)]
    h_refs = [next(it), next(it)] if has_norm else None
    i = pl.program_id(0)
    j = pl.program_id(1)

    def run(g):
        def per_seq(val, vec_ref, fn):
            if n_seq[g] == 1:
                return fn(val, vec_ref[0])
            rows, n = val.shape
            return fn(val.reshape(n_seq[g], rows // n_seq[g], n), vec_ref[...]).reshape(rows, n)

        if has_norm:
            @pl.when(j == 0)
            def _():
                xf = x_refs[g][...].astype(F32)
                ms = jnp.mean(xf * xf, axis=-1, keepdims=True)
                y = xf * lax.rsqrt(ms + EPS) * gain_ref[...]
                y = per_seq(y, sc_refs[g], lambda a, s: a * (1.0 + s))
                y = per_seq(y, sh_refs[g], lambda a, s: a + s)
                h_refs[g][...] = y.astype(BF16)
            h = h_refs[g][...]
        else:
            h = x_refs[g][...].astype(BF16)

        acc = _mm(h, w_ref[...].astype(BF16))
        if swiglu:
            up = _mm(h, wu_ref[...].astype(BF16))
            acc = _silu(acc) * up
        if has_resid:
            acc = per_seq(acc, gate_refs[g], lambda a, gt: a * (coef * gt))
            acc = res_refs[g][...] + acc
        if n_split == 1:
            o_refs[g][0][...] = acc.astype(o_refs[g][0].dtype)
        else:
            for s in range(n_split):
                @pl.when(j // cols_per_split == s)
                def _():
                    o_refs[g][s][...] = acc.astype(o_refs[g][s].dtype)

    run(0)

    @pl.when(i == pl.num_programs(0) - 1)
    def _():
        run(1)


def _linear(xs, w, w_prefix, *, n_out, tm, tn, out_dtype, seqs, mod, layer,
            norm=None, swiglu=False, resid=None, n_split=1):
    m0, k_dim = xs[0].shape
    m1 = xs[1].shape[0]
    tm = min(tm, m0)
    n_i = m0 // tm
    nj = n_out // tn
    cols_per_split = nj // n_split
    seq0_a, len_a = seqs[0]
    seq0_b, len_b = seqs[1]
    n_seq = (max(1, tm // len_a), m1 // len_b)
    tiles_per_seq = max(1, len_a // tm)
    npre = len(w_prefix)

    def seq_blk(g, i):
        if g == 1:
            return seq0_b // n_seq[1]
        if n_seq[0] == 1:
            return seq0_a + i // tiles_per_seq
        return seq0_a // n_seq[0] + i

    def col(g, i, c):
        return c if g == 0 else jnp.where(i == n_i - 1, c, 0)

    def mod_spec(g, slot, width, per_j):
        blocks = D_MODEL // width
        if per_j:
            return pl.BlockSpec((None, n_seq[g], 1, width),
                                lambda i, j: (layer, seq_blk(g, i), 0, slot * blocks + col(g, i, j)))
        return pl.BlockSpec((None, n_seq[g], 1, width),
                            lambda i, j: (layer, seq_blk(g, i), 0, slot * blocks))

    def row_spec(g, width, per_j):
        rows = tm if g == 0 else m1
        if per_j:
            return pl.BlockSpec((rows, width), lambda i, j: (i if g == 0 else 0, col(g, i, j)))
        return pl.BlockSpec((rows, width), lambda i, j: (i if g == 0 else 0, 0))

    def out_spec(g, s):
        rows = tm if g == 0 else m1
        return pl.BlockSpec(
            (rows, tn),
            lambda i, j: (i if g == 0 else 0,
                          col(g, i, jnp.clip(j - s * cols_per_split, 0, cols_per_split - 1))))

    def w_spec(col0):
        return pl.BlockSpec((None,) * npre + (k_dim, tn),
                            lambda i, j: tuple(w_prefix) + (0, col0 + j))

    args, in_specs, scratch = [], [], []
    for g in range(2):
        args.append(xs[g])
        in_specs.append(row_spec(g, k_dim, False))
        if norm is not None:
            args += [mod, mod]
            in_specs += [mod_spec(g, norm[1], D_MODEL, False), mod_spec(g, norm[2], D_MODEL, False)]
    if norm is not None:
        args.append(norm[0])
        in_specs.append(pl.BlockSpec((1, k_dim), lambda i, j: (0, 0)))
        scratch += [pltpu.VMEM((tm, k_dim), BF16), pltpu.VMEM((m1, k_dim), BF16)]
    args.append(w)
    in_specs.append(w_spec(0))
    if swiglu:
        args.append(w)
        in_specs.append(w_spec(nj))
    coef = 1.0
    if resid is not None:
        res, g_slot, coef = resid
        for g in range(2):
            args += [res[g], mod]
            in_specs += [row_spec(g, tn, True), mod_spec(g, g_slot, tn, True)]

    body = functools.partial(_linear_kernel, has_norm=norm is not None, n_seq=n_seq,
                             swiglu=swiglu, has_resid=resid is not None, coef=coef,
                             n_split=n_split, cols_per_split=cols_per_split)
    width = n_out // n_split
    outs = pl.pallas_call(
        body,
        out_shape=[jax.ShapeDtypeStruct((m, width), out_dtype)
                   for m in (m0, m1) for _ in range(n_split)],
        grid=(n_i, nj),
        in_specs=in_specs,
        out_specs=[out_spec(g, s) for g in range(2) for s in range(n_split)],
        scratch_shapes=scratch,
        compiler_params=_params("arbitrary", "arbitrary"),
        name="linear",
    )(*args)
    if n_split == 1:
        return [outs[0], outs[1]]
    return [outs[0:n_split], outs[n_split:2 * n_split]]


def _rmsnorm_kernel(x_ref, g_ref, o_ref):
    x = x_ref[...]
    ms = jnp.mean(x * x, axis=-1, keepdims=True)
    o_ref[...] = x * lax.rsqrt(ms + EPS) * g_ref[...]


def _final_norm(x, gain):
    m_rows = x.shape[0]
    tm = min(512, m_rows)
    return pl.pallas_call(
        _rmsnorm_kernel,
        out_shape=jax.ShapeDtypeStruct(x.shape, F32),
        grid=(m_rows // tm,),
        in_specs=[pl.BlockSpec((tm, D_MODEL), lambda i: (i, 0)),
                  pl.BlockSpec((1, D_MODEL), lambda i: (0, 0))],
        out_specs=pl.BlockSpec((tm, D_MODEL), lambda i: (i, 0)),
        compiler_params=_params("parallel"),
        name="final_norm",
    )(x, gain.reshape(1, D_MODEL))


def _gla_kernel(*refs, chunk, sub, n_chunks, has_state):
    if has_state:
        (q_ref, k_ref, v_ref, g_ref, r_ref, wa2_ref, ba_ref, gain_ref, s0_ref,
         o_ref, sout_ref, s_ref) = refs
    else:
        (q_ref, k_ref, v_ref, g_ref, r_ref, wa2_ref, ba_ref, gain_ref,
         o_ref, sout_ref, s_ref) = refs
    t = pl.program_id(2)
    mx = BF16 if chunk >= 16 else F32

    @pl.when(t == 0)
    def _():
        if has_state:
            s_ref[...] = s0_ref[...]
        else:
            s_ref[...] = jnp.zeros_like(s_ref)

    row = lax.broadcasted_iota(jnp.int32, (chunk, chunk), 0)
    col = lax.broadcasted_iota(jnp.int32, (chunk, chunk), 1)
    tri = (col <= row).astype(mx)
    rowi = lax.broadcasted_iota(jnp.int32, (chunk, 1), 0)
    ones = jnp.ones((chunk, LANE), mx)

    s_old = s_ref[...]
    for c in range(n_chunks):
        r0 = c * chunk
        q = q_ref[r0:r0 + chunk, :].astype(F32) * (GLA_DK ** -0.5)
        k = k_ref[r0:r0 + chunk, :].astype(F32)
        v = v_ref[r0:r0 + chunk, :].astype(mx)
        z = _mm(r_ref[r0:r0 + chunk, :].astype(mx), wa2_ref[...].astype(mx)) + ba_ref[...]
        la = (jnp.minimum(z, 0.0) - jnp.log1p(jnp.exp(-jnp.abs(z)))) * GLA_INV_TAU
        if mx == BF16:
            la_hi, la_lo = _split2(la)
            b = _mm(tri, la_hi) + _mm(tri, la_lo)
            bcol = _tn(la_hi, ones) + _tn(la_lo, ones)
        else:
            b = _mmh(tri, la)
            bcol = lax.dot_general(la, ones, (((0,), (0,)), ((), ())),
                                   preferred_element_type=F32, precision=HIGHEST)
        o = _mm((q * jnp.exp(b)).astype(mx), s_old.astype(mx))
        b_last = b[chunk - 1:chunk, :]
        kb = (k * jnp.exp(b_last - b)).astype(mx)

        parts = []
        for band in range(chunk // sub):
            lo, hi = band * sub, (band + 1) * sub
            mid = lo + sub // 2
            bref = b[mid:mid + 1, :]
            qi = (q[lo:hi] * jnp.exp(b[lo:hi] - bref)).astype(mx)
            e = jnp.where(rowi < hi, bref - b, 0.0)
            ki = (k * jnp.exp(e)).astype(mx)
            parts.append(_nt(qi, ki))
        sc = parts[0] if len(parts) == 1 else jnp.concatenate(parts, axis=0)
        sc = jnp.where(col <= row, sc, 0.0)
        o = o + _mm(sc.astype(mx), v)

        s_old = jnp.exp(bcol[:, 0:1]) * s_old + _tn(kb, v)

        ms = jnp.mean(o * o, axis=-1, keepdims=True)
        y = o * lax.rsqrt(ms + EPS) * gain_ref[...]
        o_ref[r0:r0 + chunk, :] = (y * _silu(g_ref[r0:r0 + chunk, :].astype(F32))).astype(o_ref.dtype)
    s_ref[...] = s_old

    @pl.when(t == pl.num_programs(2) - 1)
    def _():
        sout_ref[...] = s_ref[...]


def _gla_core(proj, r, wa2p, b_a, gain, s0, *, bsz, length):
    chunk = min(GLA_CHUNK, length)
    sub = min(GLA_SUB, chunk)
    tl = min(256, length)
    grid = (bsz, GLA_HEADS, length // tl)
    in_specs = [
        pl.BlockSpec((None, tl, GLA_DK), lambda b, h, t: (b, t, h)),
        pl.BlockSpec((None, tl, GLA_DK), lambda b, h, t: (b, t, GLA_HEADS + h)),
        pl.BlockSpec((None, tl, GLA_DV), lambda b, h, t: (b, t, GLA_HEADS + h)),
        pl.BlockSpec((None, tl, GLA_DV), lambda b, h, t: (b, t, 2 * GLA_HEADS + h)),
        pl.BlockSpec((None, tl, LANE), lambda b, h, t: (b, t, 0)),
        pl.BlockSpec((LANE, GLA_DK), lambda b, h, t: (0, h)),
        pl.BlockSpec((1, GLA_DK), lambda b, h, t: (0, h)),
        pl.BlockSpec((1, GLA_DV), lambda b, h, t: (0, 0)),
    ]
    args = [proj, proj, proj, proj, r, wa2p, b_a, gain]
    if s0 is not None:
        in_specs.append(pl.BlockSpec((None, None, GLA_DK, GLA_DV), lambda b, h, t: (b, h, 0, 0)))
        args.append(s0)
    body = functools.partial(_gla_kernel, chunk=chunk, sub=sub, n_chunks=tl // chunk,
                             has_state=s0 is not None)
    return pl.pallas_call(
        body,
        out_shape=(jax.ShapeDtypeStruct((bsz, length, GLA_VW), BF16),
                   jax.ShapeDtypeStruct((bsz, GLA_HEADS, GLA_DK, GLA_DV), F32)),
        grid=grid,
        in_specs=in_specs,
        out_specs=(pl.BlockSpec((None, tl, GLA_DV), lambda b, h, t: (b, t, h)),
                   pl.BlockSpec((None, None, GLA_DK, GLA_DV), lambda b, h, t: (b, h, 0, 0))),
        scratch_shapes=[pltpu.VMEM((GLA_DK, GLA_DV), F32)],
        compiler_params=_params("parallel", "parallel", "arbitrary"),
        name="gla_scan",
    )(*args)


def _diff_lambda(lam_ref, lambda_init):
    lv = lam_ref[...]
    a = jnp.sum(lv[0:1] * lv[1:2], axis=-1, keepdims=True)
    b = jnp.sum(lv[2:3] * lv[3:4], axis=-1, keepdims=True)
    return jnp.exp(a) - jnp.exp(b) + lambda_init


def _diff_finish(o, gain_ref, lambda_init):
    ms = jnp.mean(o * o, axis=-1, keepdims=True)
    return o * lax.rsqrt(ms + EPS) * gain_ref[...] * (1.0 - lambda_init)


def _diff_prompt_kernel(q_ref, k_ref, v_ref, lam_ref, gain_ref, o_ref, m_ref, l_ref, acc_ref,
                        *, tq, tk, lambda_init):
    qi = pl.program_id(2)
    kj = pl.program_id(3)

    @pl.when(kj == 0)
    def _():
        m_ref[...] = jnp.full_like(m_ref, -jnp.inf)
        l_ref[...] = jnp.zeros_like(l_ref)
        acc_ref[...] = jnp.zeros_like(acc_ref)

    def attend(diagonal):
        q = (q_ref[...] * (DIFF_DH ** -0.5)).astype(BF16)
        k = k_ref[...].astype(BF16)
        v = v_ref[...].astype(BF16)
        for m in range(2):
            sl = slice(m * DIFF_DH, (m + 1) * DIFF_DH)
            s = _nt(q[:, sl], k[:, sl])
            if diagonal:
                keep = (lax.broadcasted_iota(jnp.int32, (tq, tk), 1)
                        <= lax.broadcasted_iota(jnp.int32, (tq, tk), 0))
                s = jnp.where(keep, s, -jnp.inf)
            m_old = m_ref[m]
            m_new = jnp.maximum(m_old, jnp.max(s, axis=-1, keepdims=True))
            alpha = jnp.exp(m_old - m_new)
            p = jnp.exp(s - _lane_tile(m_new, tk // LANE))
            l_ref[m] = alpha * l_ref[m] + jnp.sum(p, axis=-1, keepdims=True)
            acc_ref[m] = _lane_tile(alpha, 2 * DIFF_DH // LANE) * acc_ref[m] + _mm(p.astype(BF16), v)
            m_ref[m] = m_new

    @pl.when(kj < qi)
    def _():
        attend(False)

    @pl.when(kj == qi)
    def _():
        attend(True)
        lam = _diff_lambda(lam_ref, lambda_init)
        reps = 2 * DIFF_DH // LANE
        o = (acc_ref[0] / _lane_tile(l_ref[0], reps)
             - lam * (acc_ref[1] / _lane_tile(l_ref[1], reps)))
        o_ref[...] = _diff_finish(o, gain_ref, lambda_init).astype(o_ref.dtype)


def _diff_prompt(q, k, v, lam_vecs, gain, *, bsz, length, lambda_init):
    tq = tk = min(512, length)
    nq = length // tq
    hw = 2 * DIFF_DH
    body = functools.partial(_diff_prompt_kernel, tq=tq, tk=tk, lambda_init=lambda_init)
    return pl.pallas_call(
        body,
        out_shape=jax.ShapeDtypeStruct((bsz, length, DIFF_W), BF16),
        grid=(bsz, DIFF_HEADS, nq, nq),
        in_specs=[
            pl.BlockSpec((None, tq, hw), lambda b, h, i, j: (b, i, h)),
            pl.BlockSpec((None, tk, hw), lambda b, h, i, j: (b, jnp.minimum(j, i), h)),
            pl.BlockSpec((None, tk, hw), lambda b, h, i, j: (b, jnp.minimum(j, i), h)),
            pl.BlockSpec((4, DIFF_DH), lambda b, h, i, j: (0, 0)),
            pl.BlockSpec((1, hw), lambda b, h, i, j: (0, 0)),
        ],
        out_specs=pl.BlockSpec((None, tq, hw), lambda b, h, i, j: (b, i, h)),
        scratch_shapes=[pltpu.VMEM((2, tq, LANE), F32), pltpu.VMEM((2, tq, LANE), F32),
                        pltpu.VMEM((2, tq, hw), F32)],
        compiler_params=_params("parallel", "parallel", "parallel", "arbitrary"),
        name="diff_attn_prompt",
    )(q, k, v, lam_vecs, gain)


def _diff_decode_kernel(pt_ref, q_ref, *refs, n_steps, pps, n_new, lambda_init):
    del pt_ref
    kc_refs, vc_refs = refs[0:pps], refs[pps:2 * pps]
    (kn_ref, vn_ref, lam_ref, gain_ref, o_ref,
     qt_ref, bias_ref, m_ref, l_ref, acc_ref) = refs[2 * pps:]
    b = pl.program_id(0)
    j = pl.program_id(1)
    nrow = 2 * DIFF_HEADS * n_new
    hw = 2 * DIFF_DH
    page_cols = PAGE_SIZE * DIFF_HEADS

    def head_match(ncols):
        r = lax.broadcasted_iota(jnp.int32, (nrow, ncols), 0)
        c = lax.broadcasted_iota(jnp.int32, (nrow, ncols), 1)
        return r, c, (c % DIFF_HEADS) == ((r // n_new) % DIFF_HEADS)

    @pl.when((b == 0) & (j == 0))
    def _():
        _, _, same = head_match(page_cols)
        bias_ref[...] = jnp.where(same, 0.0, -jnp.inf)

    @pl.when(j == 0)
    def _():
        qt_ref[...] = jnp.zeros_like(qt_ref)
        for m in range(2):
            for h in range(DIFF_HEADS):
                r0 = (m * DIFF_HEADS + h) * n_new
                c0 = h * hw + m * DIFF_DH
                qt_ref[r0:r0 + n_new, m * DIFF_DH:(m + 1) * DIFF_DH] = (
                    q_ref[:, c0:c0 + DIFF_DH] * (DIFF_DH ** -0.5))
        m_ref[...] = jnp.full_like(m_ref, -jnp.inf)
        l_ref[...] = jnp.zeros_like(l_ref)
        acc_ref[...] = jnp.zeros_like(acc_ref)

    def attend(kvs, bias):
        qt = qt_ref[...].astype(BF16)
        scores = [_nt(qt, k2d.astype(BF16)) + bias for k2d, _ in kvs]
        ncols = bias.shape[1]
        m_old = m_ref[...]
        m_new = m_old
        for s in scores:
            m_new = jnp.maximum(m_new, jnp.max(s, axis=-1, keepdims=True))
        alpha = jnp.exp(m_old - m_new)
        l_new = alpha * l_ref[...]
        acc = _lane_tile(alpha, hw // LANE) * acc_ref[...]
        m_wide = m_new[:, 0:ncols] if ncols <= LANE else _lane_tile(m_new, ncols // LANE)
        for s, (_, v2d) in zip(scores, kvs):
            p = jnp.exp(s - m_wide)
            l_new = l_new + jnp.sum(p, axis=-1, keepdims=True)
            acc = acc + _mm(p.astype(BF16), v2d.astype(BF16))
        l_ref[...] = l_new
        acc_ref[...] = acc
        m_ref[...] = m_new

    @pl.when(j < n_steps)
    def _():
        attend([(kr[...].reshape(page_cols, hw), vr[...].reshape(page_cols, hw))
                for kr, vr in zip(kc_refs, vc_refs)], bias_ref[...])

    @pl.when(j == n_steps)
    def _():
        r, c, same = head_match(n_new * DIFF_HEADS)
        keep = same & ((c // DIFF_HEADS) <= (r % n_new))
        attend([(kn_ref[...], vn_ref[...])], jnp.where(keep, 0.0, -jnp.inf))
        lam = _diff_lambda(lam_ref, lambda_init)
        o2 = acc_ref[...] / _lane_tile(l_ref[...], hw // LANE)
        half = nrow // 2
        o = o2[0:half] - lam * o2[half:nrow]
        y = _diff_finish(o, gain_ref, lambda_init)
        for h in range(DIFF_HEADS):
            o_ref[:, h * hw:(h + 1) * hw] = y[h * n_new:(h + 1) * n_new, :].astype(o_ref.dtype)


def _diff_decode(q, k_new, v_new, cache_k, cache_v, page_table, layer, lam_vecs, gain,
                 *, bsz, n_new, lambda_init):
    n_pages = page_table.shape[1]
    pps = 4 if n_pages % 4 == 0 else (2 if n_pages % 2 == 0 else 1)
    n_steps = n_pages // pps
    hw = 2 * DIFF_DH
    nrow = 2 * DIFF_HEADS * n_new
    page_blk = (None, None, PAGE_SIZE, DIFF_HEADS, hw)

    def page_spec(p):
        return pl.BlockSpec(
            page_blk,
            lambda b, j, pt: (layer, pt[b, jnp.minimum(j, n_steps - 1) * pps + p], 0, 0, 0))

    body = functools.partial(_diff_decode_kernel, n_steps=n_steps, pps=pps, n_new=n_new,
                             lambda_init=lambda_init)
    new_spec = pl.BlockSpec((None, n_new * DIFF_HEADS, hw), lambda b, j, pt: (b, 0, 0))
    grid_spec = pltpu.PrefetchScalarGridSpec(
        num_scalar_prefetch=1,
        grid=(bsz, n_steps + 1),
        in_specs=(
            [pl.BlockSpec((None, n_new, DIFF_W), lambda b, j, pt: (b, 0, 0))]
            + [page_spec(p) for p in range(pps)] * 2
            + [new_spec, new_spec,
               pl.BlockSpec((4, DIFF_DH), lambda b, j, pt: (0, 0)),
               pl.BlockSpec((1, hw), lambda b, j, pt: (0, 0))]),
        out_specs=pl.BlockSpec((None, n_new, DIFF_W), lambda b, j, pt: (b, 0, 0)),
        scratch_shapes=[
            pltpu.VMEM((nrow, hw), F32),
            pltpu.VMEM((nrow, PAGE_SIZE * DIFF_HEADS), F32),
            pltpu.VMEM((nrow, LANE), F32),
            pltpu.VMEM((nrow, LANE), F32),
            pltpu.VMEM((nrow, hw), F32),
        ],
    )
    return pl.pallas_call(
        body,
        out_shape=jax.ShapeDtypeStruct((bsz, n_new, DIFF_W), BF16),
        grid_spec=grid_spec,
        compiler_params=_params("arbitrary", "arbitrary"),
        name="diff_attn_decode",
    )(page_table, q, *([cache_k] * pps), *([cache_v] * pps), k_new, v_new, lam_vecs, gain)


def _gdn_prep_kernel(x_ref, buf_ref, w_ref, o_ref, nb_ref, *, length, tc, nq_blocks, nqk_blocks):
    j = pl.program_id(1)
    x = x_ref[...].astype(F32)
    buf = buf_ref[...]
    w = w_ref[...]
    b0, b1, b2 = buf[0:1], buf[1:2], buf[2:3]
    head = 8

    def conv(x0, x1, x2, x3):
        return _silu(w[3:4] * x0 + w[2:3] * x1 + w[1:2] * x2 + w[0:1] * x3)

    def emit(rows, y):
        @pl.when(j < nqk_blocks)
        def _():
            scale = jnp.where(j < nq_blocks, GDN_DK ** -0.5, 1.0)
            for h in range(tc // GDN_DK):
                sl = slice(h * GDN_DK, (h + 1) * GDN_DK)
                yh = y[:, sl]
                ss = jnp.sum(yh * yh, axis=-1, keepdims=True)
                o_ref[rows, sl] = (yh * (lax.rsqrt(ss + EPS) * scale)).astype(o_ref.dtype)

        @pl.when(j >= nqk_blocks)
        def _():
            o_ref[rows, :] = y.astype(o_ref.dtype)

    if length > head:
        emit(slice(0, length),
             conv(x, pltpu.roll(x, 1, 0), pltpu.roll(x, 2, 0), pltpu.roll(x, 3, 0)))
    xt = x[0:head]
    row = lax.broadcasted_iota(jnp.int32, (head, 1), 0)
    x1 = jnp.where(row >= 1, pltpu.roll(xt, 1, 0), b2)
    x2 = jnp.where(row >= 2, pltpu.roll(xt, 2, 0), jnp.where(row == 1, b2, b1))
    x3 = jnp.where(row >= 3, pltpu.roll(xt, 3, 0),
                   jnp.where(row == 2, b2, jnp.where(row == 1, b1, b0)))
    emit(slice(0, head), conv(xt, x1, x2, x3))
    nb_ref[...] = x[length - (GDN_CONV - 1):length, :]


def _gdn_prep(proj, buf, conv_w, *, bsz, length):
    tc = 512
    body = functools.partial(_gdn_prep_kernel, length=length, tc=tc,
                             nq_blocks=GDN_KW // tc, nqk_blocks=2 * GDN_KW // tc)
    return pl.pallas_call(
        body,
        out_shape=(jax.ShapeDtypeStruct((bsz, length, GDN_QKV_W), BF16),
                   jax.ShapeDtypeStruct((bsz, GDN_CONV - 1, GDN_QKV_W), F32)),
        grid=(bsz, GDN_QKV_W // tc),
        in_specs=[
            pl.BlockSpec((None, length, tc), lambda b, j: (b, 0, j)),
            pl.BlockSpec((None, GDN_CONV - 1, tc), lambda b, j: (b, 0, j)),
            pl.BlockSpec((GDN_CONV, tc), lambda b, j: (0, j)),
        ],
        out_specs=(pl.BlockSpec((None, length, tc), lambda b, j: (b, 0, j)),
                   pl.BlockSpec((None, GDN_CONV - 1, tc), lambda b, j: (b, 0, j))),
        compiler_params=_params("parallel", "parallel"),
        name="gdn_conv_prep",
    )(proj, buf, conv_w)


def _gdn_kernel(*refs, chunk, n_chunks, has_state):
    if has_state:
        (q_ref, k_ref, v_ref, z_ref, bg_ref, alog_ref, dtb_ref, gain_ref, s0_ref,
         o_ref, sout_ref, s_ref) = refs
    else:
        (q_ref, k_ref, v_ref, z_ref, bg_ref, alog_ref, dtb_ref, gain_ref,
         o_ref, sout_ref, s_ref) = refs
    t = pl.program_id(2)
    big = chunk >= 16
    mx = BF16 if big else F32
    rep = GDN_GV // GDN_GQ

    @pl.when(t == 0)
    def _():
        if has_state:
            s_ref[...] = s0_ref[...]
        else:
            s_ref[...] = jnp.zeros_like(s_ref)

    row = lax.broadcasted_iota(jnp.int32, (chunk, chunk), 0)
    col = lax.broadcasted_iota(jnp.int32, (chunk, chunk), 1)
    lower = col <= row
    strict = col < row
    eye_f = (col == row).astype(F32)

    def bmm(a, b):
        return lax.dot_general(a, b, (((2,), (1,)), ((0,), (0,))), preferred_element_type=F32)

    def bmm_f32(a, b, b_exact=True):
        if not big:
            return lax.dot_general(a, b, (((2,), (1,)), ((0,), (0,))),
                                   preferred_element_type=F32, precision=HIGHEST)
        a_hi, a_lo = _split2(a)
        if not b_exact:
            b_hi = b.astype(BF16)
            return bmm(a_hi, b_hi) + bmm(a_lo, b_hi)
        b_hi, b_lo = _split2(b)
        return bmm(a_hi, b_hi) + (bmm(a_lo, b_hi) + bmm(a_hi, b_lo))

    def bnt(a, b):
        return lax.dot_general(a, b, (((2,), (2,)), ((0,), (0,))), preferred_element_type=F32)

    heads = range(GDN_GV)

    def sub_block(size):
        return (((row // (2 * size)) == (col // (2 * size)))
                & ((row // size) % 2 == 1) & ((col // size) % 2 == 0))[None]

    def state_free(r0):
        rows = slice(r0, r0 + chunk)
        bg = bg_ref[rows, :]
        beta_all = jax.nn.sigmoid(bg[:, 0:GDN_GV])
        a_in = bg[:, GDN_GV:2 * GDN_GV] + dtb_ref[...]
        softplus = jnp.maximum(a_in, 0.0) + jnp.log1p(jnp.exp(-jnp.abs(a_in)))
        g_all = -jnp.exp(alog_ref[...]) * softplus
        if big:
            g1 = g_all.astype(BF16)
            r1 = g_all - g1.astype(F32)
            g2 = r1.astype(BF16)
            g3 = (r1 - g2.astype(F32)).astype(BF16)
            low_b = lower.astype(BF16)
            up_b = (row <= col).astype(BF16)
            gc_all = _mm(low_b, g1) + (_mm(low_b, g2) + _mm(low_b, g3))
            gr_all = _tn(g1, up_b) + (_tn(g2, up_b) + _tn(g3, up_b))
        else:
            gc_all = _mmh(lower.astype(F32), g_all)
            gr_all = lax.dot_general(g_all, (row <= col).astype(F32), (((0,), (0,)), ((), ())),
                                     preferred_element_type=F32, precision=HIGHEST)

        gcol = jnp.stack([gc_all[:, h:h + 1] for h in heads])
        grow = jnp.stack([gr_all[h:h + 1, :] for h in heads])
        beta = jnp.stack([beta_all[:, h:h + 1] for h in heads])
        q4 = jnp.stack([q_ref[rows, h * GDN_DK:(h + 1) * GDN_DK].astype(F32)
                        for h in range(GDN_GQ)])
        k4 = jnp.stack([k_ref[rows, h * GDN_DK:(h + 1) * GDN_DK].astype(F32)
                        for h in range(GDN_GQ)])
        kk4 = bnt(k4.astype(mx), k4.astype(mx))
        qk4 = bnt(q4.astype(mx), k4.astype(mx))
        kk = jnp.stack([kk4[h // rep] for h in heads])
        qk = jnp.stack([qk4[h // rep] for h in heads])
        k8 = jnp.stack([k4[h // rep] for h in heads])
        v8 = jnp.stack([v_ref[rows, h * GDN_DV:(h + 1) * GDN_DV].astype(F32) for h in heads])

        decay = jnp.exp(jnp.where(lower[None], gcol - grow, -jnp.inf))
        m = jnp.where(strict[None], beta * kk * decay, 0.0)

        inv = eye_f[None] - jnp.where(sub_block(1), m, 0.0)
        size = 2
        while size < chunk:
            low = jnp.where(sub_block(size), m, 0.0)
            inv = inv - bmm_f32(bmm_f32(inv, low, b_exact=False), inv)
            size *= 2

        eg = jnp.exp(gcol)
        rhs = jnp.concatenate([beta * v8, (beta * eg) * k8], axis=-1)
        sol = bmm_f32(inv, rhs, b_exact=False)
        attn = jnp.where(lower[None], qk * decay, 0.0).astype(mx)
        g_last = gcol[:, chunk - 1:chunk, :]
        kd = (k8 * jnp.exp(g_last - gcol)).astype(mx)
        qe = (jnp.stack([q4[h // rep] for h in heads]) * eg).astype(mx)
        return sol, attn, kd, qe, jnp.exp(g_last)

    def recur(r0, parts, states):
        sol, attn, kd, qe, s_scale = parts
        rows = slice(r0, r0 + chunk)
        new_states = []
        for h in heads:
            vs = slice(h * GDN_DV, (h + 1) * GDN_DV)
            s_old = states[h]
            s_mx = s_old.astype(mx)
            delta = sol[h, :, 0:GDN_DV] - _mm(sol[h, :, GDN_DV:].astype(mx), s_mx)
            o = _mm(qe[h], s_mx) + _mm(attn[h], delta.astype(mx))
            new_states.append(s_scale[h] * s_old + _tn(kd[h], delta.astype(mx)))
            ms = jnp.mean(o * o, axis=-1, keepdims=True)
            y = o * lax.rsqrt(ms + EPS) * gain_ref[...]
            o_ref[rows, vs] = (y * _silu(z_ref[rows, vs].astype(F32))).astype(o_ref.dtype)
        return new_states

    parts = [state_free(c * chunk) for c in range(n_chunks)]
    states = [s_ref[h] for h in heads]
    for c in range(n_chunks):
        states = recur(c * chunk, parts[c], states)
    for h in heads:
        s_ref[h] = states[h]

    @pl.when(t == pl.num_programs(2) - 1)
    def _():
        sout_ref[...] = s_ref[...]


def _gdn_core(qkv, proj, bg, a_log, dt_bias, gain, s0, *, bsz, length):
    chunk = min(GDN_CHUNK, length)
    tl = min(2 * GDN_CHUNK, length)
    qw = GDN_GQ * GDN_DK
    vw = GDN_GV * GDN_DV
    in_specs = [
        pl.BlockSpec((None, tl, qw), lambda b, g, t: (b, t, g)),
        pl.BlockSpec((None, tl, qw), lambda b, g, t: (b, t, GDN_KW // qw + g)),
        pl.BlockSpec((None, tl, vw), lambda b, g, t: (b, t, 2 * GDN_KW // vw + g)),
        pl.BlockSpec((None, tl, vw), lambda b, g, t: (b, t, GDN_QKV_W // vw + g)),
        pl.BlockSpec((None, None, tl, 2 * GDN_GV), lambda b, g, t: (b, g, t, 0)),
        pl.BlockSpec((None, 1, GDN_GV), lambda b, g, t: (g, 0, 0)),
        pl.BlockSpec((None, 1, GDN_GV), lambda b, g, t: (g, 0, 0)),
        pl.BlockSpec((1, GDN_DV), lambda b, g, t: (0, 0)),
    ]
    args = [qkv, qkv, qkv, proj, bg, a_log, dt_bias, gain]
    if s0 is not None:
        in_specs.append(pl.BlockSpec((None, GDN_GV, GDN_DK, GDN_DV), lambda b, g, t: (b, g, 0, 0)))
        args.append(s0)
    body = functools.partial(_gdn_kernel, chunk=chunk, n_chunks=tl // chunk,
                             has_state=s0 is not None)
    return pl.pallas_call(
        body,
        out_shape=(jax.ShapeDtypeStruct((bsz, length, GDN_VW), BF16),
                   jax.ShapeDtypeStruct((bsz, GDN_V_HEADS, GDN_DK, GDN_DV), F32)),
        grid=(bsz, GDN_GROUPS, length // tl),
        in_specs=in_specs,
        out_specs=(pl.BlockSpec((None, tl, vw), lambda b, g, t: (b, t, g)),
                   pl.BlockSpec((None, GDN_GV, GDN_DK, GDN_DV), lambda b, g, t: (b, g, 0, 0))),
        scratch_shapes=[pltpu.VMEM((GDN_GV, GDN_DK, GDN_DV), F32)],
        compiler_params=_params("parallel", "parallel", "arbitrary"),
        name="gdn_scan",
    )(*args)


def _trunk(x_prompt, x_sample, w, past, mod, n_s):
    dims = [x_prompt.shape[0:2], x_sample.shape[0:2]]
    x = [x_prompt.reshape(-1, D_MODEL), x_sample.reshape(-1, D_MODEL)]
    seqs = [(n_s, dims[0][1]), (0, dims[1][1])]
    lin = functools.partial(_linear, seqs=seqs, mod=mod, tm=1024)
    groups = range(2)
    new_gla, new_dk, new_dv, new_gs, new_gc = ([[], []] for _ in range(5))

    def ffn(x, i, which, slots):
        sh, sc, g = slots
        gain = w['norm_gain'][i, 2 * which].reshape(1, D_MODEL)
        act = lin(x, w['ffn_w_up'], (i, which), n_out=D_FF, tn=512, out_dtype=BF16, layer=i,
                  norm=(gain, sh, sc), swiglu=True)
        return lin(act, w['ffn_w_down'], (i, which), n_out=D_MODEL, tn=256, out_dtype=F32,
                   layer=i, resid=(x, g, 0.5))

    for i in range(DEPTH):
        kind, j = i % 3, i // 3
        x = ffn(x, i, 0, (0, 1, 2))
        gain2 = w['norm_gain'][i, 1].reshape(1, D_MODEL)
        norm2 = (gain2, 3, 4)
        o = []
        if kind == 0:
            proj = lin(x, w['gla_w_in'], (j,), n_out=2 * GLA_KW + 2 * GLA_VW, tn=1024,
                       out_dtype=BF16, layer=i, norm=norm2)
            wa1p = jnp.pad(w['gla_w_a1'][j], ((0, 0), (0, LANE - GLA_RANK)))
            r = lin(x, wa1p, (), n_out=LANE, tn=LANE, out_dtype=F32, layer=i, norm=norm2)
            wa2p = jnp.pad(w['gla_w_a2'][j], ((0, LANE - GLA_RANK), (0, 0)))
            for g in groups:
                bsz, length = dims[g]
                s0 = None if g == 0 else past['state_gla'][j]
                og, s = _gla_core(proj[g].reshape(bsz, length, -1), r[g].reshape(bsz, length, LANE),
                                  wa2p, w['gla_b_a'][j].reshape(1, GLA_KW),
                                  w['gla_norm'][j].reshape(1, GLA_DV), s0, bsz=bsz, length=length)
                o.append(og)
                new_gla[g].append(s)
            w_o, w_o_pre = w['gla_w_o'], (j,)
        elif kind == 1:
            lambda_init = 0.8 - 0.6 * math.exp(-0.3 * i)
            qkv = lin(x, w['diff_w_in'], (j,), n_out=3 * DIFF_W, tn=512, out_dtype=F32,
                      layer=i, norm=norm2, n_split=3)
            gain = w['diff_norm'][j].reshape(1, 2 * DIFF_DH)
            for g in groups:
                bsz, length = dims[g]
                q, k, v = (t.reshape(bsz, length, DIFF_W) for t in qkv[g])
                if g == 0:
                    og = _diff_prompt(q, k, v, w['diff_lambda'][j], gain, bsz=bsz, length=length,
                                      lambda_init=lambda_init)
                else:
                    og = _diff_decode(q, k.reshape(bsz, length * DIFF_HEADS, 2 * DIFF_DH),
                                      v.reshape(bsz, length * DIFF_HEADS, 2 * DIFF_DH),
                                      past['cache_diff_k'], past['cache_diff_v'],
                                      past['page_table'], j, w['diff_lambda'][j], gain, bsz=bsz,
                                      n_new=length, lambda_init=lambda_init)
                o.append(og)
                new_dk[g].append(k.reshape(bsz, length, DIFF_HEADS, 2 * DIFF_DH))
                new_dv[g].append(v.reshape(bsz, length, DIFF_HEADS, 2 * DIFF_DH))
            w_o, w_o_pre = w['diff_w_o'], (j,)
        else:
            main_w = GDN_QKV_W + GDN_VW
            proj = lin(x, w['gdn_w_in'][j], (), n_out=main_w, tn=1024, out_dtype=BF16, layer=i,
                       norm=norm2)
            w_ba = jnp.pad(w['gdn_w_in'][j][:, main_w:], ((0, 0), (0, LANE - 2 * GDN_V_HEADS)))
            ba = lin(x, w_ba, (), n_out=LANE, tn=LANE, out_dtype=F32, layer=i, norm=norm2)
            for g in groups:
                bsz, length = dims[g]
                bg = ba[g].reshape(bsz, length, LANE)[:, :, 0:2 * GDN_V_HEADS]
                bg = bg.reshape(bsz, length, 2, GDN_GROUPS, GDN_GV)
                bg = jnp.transpose(bg, (0, 3, 1, 2, 4)).reshape(bsz, GDN_GROUPS, length, 2 * GDN_GV)
                proj3 = proj[g].reshape(bsz, length, main_w)
                if g == 0:
                    buf = jnp.zeros((bsz, GDN_CONV - 1, GDN_QKV_W), F32)
                    s0 = None
                else:
                    buf = past['state_gdn_conv'][j]
                    s0 = past['state_gdn'][j]
                qkv, nbuf = _gdn_prep(proj3, buf, w['gdn_conv_w'][j], bsz=bsz, length=length)
                og, s = _gdn_core(qkv, proj3, bg,
                                  w['gdn_a_log'][j].reshape(GDN_GROUPS, 1, GDN_GV),
                                  w['gdn_dt_bias'][j].reshape(GDN_GROUPS, 1, GDN_GV),
                                  w['gdn_norm'][j].reshape(1, GDN_DV), s0, bsz=bsz, length=length)
                o.append(og)
                new_gs[g].append(s)
                new_gc[g].append(nbuf)
            w_o, w_o_pre = w['gdn_w_o'], (j,)
        o = [o[g].reshape(x[g].shape[0], -1) for g in groups]
        tn_o = 1024 if w_o.shape[-2] <= D_MODEL else 512
        x = lin(o, w_o, w_o_pre, n_out=D_MODEL, tn=tn_o, out_dtype=F32, layer=i,
                resid=(x, 5, 1.0))
        x = ffn(x, i, 1, (6, 7, 8))
    y = [_final_norm(x[g], w['final_gain']).reshape(*dims[g], D_MODEL) for g in groups]
    return y, new_gla, new_dk, new_dv, new_gs, new_gc


def kernel(x_prompt, x_sample, cache_diff_k, cache_diff_v, state_gla, state_gdn, state_gdn_conv,
           page_table, c_prompt, c_sample, ada_w, ada_b, norm_gain, final_gain, ffn_w_up,
           ffn_w_down, gla_w_in, gla_w_a1, gla_w_a2, gla_b_a, gla_norm, gla_w_o, diff_w_in,
           diff_lambda, diff_norm, diff_w_o, gdn_w_in, gdn_conv_w, gdn_a_log, gdn_dt_bias,
           gdn_norm, gdn_w_o):
    w = {
        'norm_gain': norm_gain, 'final_gain': final_gain,
        'ffn_w_up': ffn_w_up, 'ffn_w_down': ffn_w_down,
        'gla_w_in': gla_w_in, 'gla_w_a1': gla_w_a1, 'gla_w_a2': gla_w_a2, 'gla_b_a': gla_b_a,
        'gla_norm': gla_norm, 'gla_w_o': gla_w_o,
        'diff_w_in': diff_w_in, 'diff_lambda': diff_lambda, 'diff_norm': diff_norm,
        'diff_w_o': diff_w_o,
        'gdn_w_in': gdn_w_in, 'gdn_conv_w': gdn_conv_w, 'gdn_a_log': gdn_a_log,
        'gdn_dt_bias': gdn_dt_bias, 'gdn_norm': gdn_norm, 'gdn_w_o': gdn_w_o,
    }
    n_s, n_p = c_sample.shape[0], c_prompt.shape[0]
    n_rows = 16
    c_all = jnp.concatenate(
        [c_sample, c_prompt, jnp.zeros((n_rows - n_s - n_p, D_MODEL), F32)], axis=0)
    mod = _ada(c_all, ada_w, ada_b).reshape(DEPTH, n_rows, 1, N_MOD * D_MODEL)

    past = {
        'state_gla': state_gla, 'cache_diff_k': cache_diff_k, 'cache_diff_v': cache_diff_v,
        'page_table': page_table, 'state_gdn': state_gdn, 'state_gdn_conv': state_gdn_conv,
    }
    y, gla, dk, dv, gdn, conv = _trunk(x_prompt, x_sample, w, past, mod, n_s)
    return (y[0], y[1],
            jnp.stack(gla[0]), jnp.stack(gla[1]),
            jnp.stack(dk[0]), jnp.stack(dv[0]), jnp.stack(dk[1]), jnp.stack(dv[1]),
            jnp.stack(gdn[0]), jnp.stack(gdn[1]),
            jnp.stack(conv[0]), jnp.stack(conv[1]))
```

```python
import functools
import math

import jax
import jax.numpy as jnp
from jax import lax
from jax.experimental import pallas as pl
from jax.experimental.pallas import tpu as pltpu

F32 = jnp.float32
BF16 = jnp.bfloat16
HIGHEST = lax.Precision.HIGHEST

D_MODEL = 2048
DEPTH = 4
EPS = 1e-6
N_MOD = 9
D_FF = 5632

GLA_HEADS = 4
GLA_DK = 256
GLA_DV = 512
GLA_KW = GLA_HEADS * GLA_DK
GLA_VW = GLA_HEADS * GLA_DV
GLA_RANK = 16
GLA_INV_TAU = 1.0 / 16.0
GLA_CHUNK = 64
GLA_SUB = 16

DIFF_DH = 128
DIFF_HEADS = 8
DIFF_W = 2048
PAGE_SIZE = 128

GDN_DK = 128
GDN_DV = 128
GDN_QK_HEADS = 16
GDN_V_HEADS = 32
GDN_KW = 2048
GDN_VW = 4096
GDN_QKV_W = 8192
GDN_CONV = 4
GDN_CHUNK = 64
GDN_GROUPS = 4
GDN_GQ = GDN_QK_HEADS // GDN_GROUPS
GDN_GV = GDN_V_HEADS // GDN_GROUPS

LANE = 128
VMEM_LIMIT = 56 * 1024 * 1024


def _params(*sem):
    return pltpu.CompilerParams(dimension_semantics=sem, vmem_limit_bytes=VMEM_LIMIT)


def _nt(a, b):
    return lax.dot_general(a, b, (((1,), (1,)), ((), ())), preferred_element_type=F32)


def _tn(a, b):
    return lax.dot_general(a, b, (((0,), (0,)), ((), ())), preferred_element_type=F32)


def _mm(a, b):
    return jnp.dot(a, b, preferred_element_type=F32)


def _mmh(a, b):
    return jnp.dot(a, b, preferred_element_type=F32, precision=HIGHEST)


def _silu(x):
    return x * jax.nn.sigmoid(x)


def _lane_tile(x, n):
    return x if n == 1 else jnp.concatenate([x] * n, axis=1)


def _split2(x):
    hi = x.astype(BF16)
    lo = (x - hi.astype(F32)).astype(BF16)
    return hi, lo


def _ada_kernel(c_ref, w_ref, b_ref, o_ref):
    s = _silu(c_ref[...]).astype(BF16)
    o_ref[...] = _mm(s, w_ref[...].astype(BF16)) + b_ref[...]


def _ada(c_all, ada_w, ada_b):
    rows = c_all.shape[0]
    width = ada_w.shape[2]
    tn = 1024
    return pl.pallas_call(
        _ada_kernel,
        out_shape=jax.ShapeDtypeStruct((DEPTH, rows, width), F32),
        grid=(DEPTH, width // tn),
        in_specs=[
            pl.BlockSpec((rows, D_MODEL), lambda l, j: (0, 0)),
            pl.BlockSpec((None, D_MODEL, tn), lambda l, j: (l, 0, j)),
            pl.BlockSpec((None, 1, tn), lambda l, j: (l, 0, j)),
        ],
        out_specs=pl.BlockSpec((None, rows, tn), lambda l, j: (l, 0, j)),
        compiler_params=_params("parallel", "parallel"),
        name="ada_mod",
    )(c_all, ada_w, ada_b.reshape(DEPTH, 1, width))


def _linear_kernel(*refs, has_norm, n_seq, swiglu, has_resid, coef, n_split, cols_per_split):
    it = iter(refs)
    x_refs, sh_refs, sc_refs = [], [], []
    for _ in range(2):
        x_refs.append(next(it))
        if has_norm:
            sh_refs.append(next(it))
            sc_refs.append(next(it))
    gain_ref = next(it) if has_norm else None
    w_ref = next(it)
    wu_ref = next(it) if swiglu else None
    res_refs, gate_refs = [], []
    if has_resid:
        for _ in range(2):
            res_refs.append(next(it))
            gate_refs.append(next(it))
    o_refs = [[next(it) for _ in range(n_split)] for _ in range(2)]
    h_refs = [next(it), next(it)] if has_norm else None
    i = pl.program_id(0)
    j = pl.program_id(1)

    def run(g):
        def per_seq(val, vec_ref, fn):
            if n_seq[g] == 1:
                return fn(val, vec_ref[0])
            rows, n = val.shape
            return fn(val.reshape(n_seq[g], rows // n_seq[g], n), vec_ref[...]).reshape(rows, n)

        if has_norm:
            @pl.when(j == 0)
            def _():
                xf = x_refs[g][...].astype(F32)
                ms = jnp.mean(xf * xf, axis=-1, keepdims=True)
                y = xf * lax.rsqrt(ms + EPS) * gain_ref[...]
                y = per_seq(y, sc_refs[g], lambda a, s: a * (1.0 + s))
                y = per_seq(y, sh_refs[g], lambda a, s: a + s)
                h_refs[g][...] = y.astype(BF16)
            h = h_refs[g][...]
        else:
            h = x_refs[g][...].astype(BF16)

        acc = _mm(h, w_ref[...].astype(BF16))
        if swiglu:
            up = _mm(h, wu_ref[...].astype(BF16))
            acc = _silu(acc) * up
        if has_resid:
            acc = per_seq(acc, gate_refs[g], lambda a, gt: a * (coef * gt))
            acc = res_refs[g][...] + acc
        if n_split == 1:
            o_refs[g][0][...] = acc.astype(o_refs[g][0].dtype)
        else:
            for s in range(n_split):
                @pl.when(j // cols_per_split == s)
                def _():
                    o_refs[g][s][...] = acc.astype(o_refs[g][s].dtype)

    run(0)

    @pl.when(i == pl.num_programs(0) - 1)
    def _():
        run(1)


def _linear(xs, w, w_prefix, *, n_out, tm, tn, out_dtype, seqs, mod, layer,
            norm=None, swiglu=False, resid=None, n_split=1):
    m0, k_dim = xs[0].shape
    m1 = xs[1].shape[0]
    tm = min(tm, m0)
    n_i = m0 // tm
    nj = n_out // tn
    cols_per_split = nj // n_split
    seq0_a, len_a = seqs[0]
    seq0_b, len_b = seqs[1]
    n_seq = (max(1, tm // len_a), m1 // len_b)
    tiles_per_seq = max(1, len_a // tm)
    npre = len(w_prefix)

    def seq_blk(g, i):
        if g == 1:
            return seq0_b // n_seq[1]
        if n_seq[0] == 1:
            return seq0_a + i // tiles_per_seq
        return seq0_a // n_seq[0] + i

    def col(g, i, c):
        return c if g == 0 else jnp.where(i == n_i - 1, c, 0)

    def mod_spec(g, slot, width, per_j):
        blocks = D_MODEL // width
        if per_j:
            return pl.BlockSpec((None, n_seq[g], 1, width),
                                lambda i, j: (layer, seq_blk(g, i), 0, slot * blocks + col(g, i, j)))
        return pl.BlockSpec((None, n_seq[g], 1, width),
                            lambda i, j: (layer, seq_blk(g, i), 0, slot * blocks))

    def row_spec(g, width, per_j):
        rows = tm if g == 0 else m1
        if per_j:
            return pl.BlockSpec((rows, width), lambda i, j: (i if g == 0 else 0, col(g, i, j)))
        return pl.BlockSpec((rows, width), lambda i, j: (i if g == 0 else 0, 0))

    def out_spec(g, s):
        rows = tm if g == 0 else m1
        return pl.BlockSpec(
            (rows, tn),
            lambda i, j: (i if g == 0 else 0,
                          col(g, i, jnp.clip(j - s * cols_per_split, 0, cols_per_split - 1))))

    def w_spec(col0):
        return pl.BlockSpec((None,) * npre + (k_dim, tn),
                            lambda i, j: tuple(w_prefix) + (0, col0 + j))

    args, in_specs, scratch = [], [], []
    for g in range(2):
        args.append(xs[g])
        in_specs.append(row_spec(g, k_dim, False))
        if norm is not None:
            args += [mod, mod]
            in_specs += [mod_spec(g, norm[1], D_MODEL, False), mod_spec(g, norm[2], D_MODEL, False)]
    if norm is not None:
        args.append(norm[0])
        in_specs.append(pl.BlockSpec((1, k_dim), lambda i, j: (0, 0)))
        scratch += [pltpu.VMEM((tm, k_dim), BF16), pltpu.VMEM((m1, k_dim), BF16)]
    args.append(w)
    in_specs.append(w_spec(0))
    if swiglu:
        args.append(w)
        in_specs.append(w_spec(nj))
    coef = 1.0
    if resid is not None:
        res, g_slot, coef = resid
        for g in range(2):
            args += [res[g], mod]
            in_specs += [row_spec(g, tn, True), mod_spec(g, g_slot, tn, True)]

    body = functools.partial(_linear_kernel, has_norm=norm is not None, n_seq=n_seq,
                             swiglu=swiglu, has_resid=resid is not None, coef=coef,
                             n_split=n_split, cols_per_split=cols_per_split)
    width = n_out // n_split
    outs = pl.pallas_call(
        body,
        out_shape=[jax.ShapeDtypeStruct((m, width), out_dtype)
                   for m in (m0, m1) for _ in range(n_split)],
        grid=(n_i, nj),
        in_specs=in_specs,
        out_specs=[out_spec(g, s) for g in range(2) for s in range(n_split)],
        scratch_shapes=scratch,
        compiler_params=_params("arbitrary", "arbitrary"),
        name="linear",
    )(*args)
    if n_split == 1:
        return [outs[0], outs[1]]
    return [outs[0:n_split], outs[n_split:2 * n_split]]


def _rmsnorm_kernel(x_ref, g_ref, o_ref):
    x = x_ref[...]
    ms = jnp.mean(x * x, axis=-1, keepdims=True)
    o_ref[...] = x * lax.rsqrt(ms + EPS) * g_ref[...]


def _final_norm(x, gain):
    m_rows = x.shape[0]
    tm = min(512, m_rows)
    return pl.pallas_call(
        _rmsnorm_kernel,
        out_shape=jax.ShapeDtypeStruct(x.shape, F32),
        grid=(m_rows // tm,),
        in_specs=[pl.BlockSpec((tm, D_MODEL), lambda i: (i, 0)),
                  pl.BlockSpec((1, D_MODEL), lambda i: (0, 0))],
        out_specs=pl.BlockSpec((tm, D_MODEL), lambda i: (i, 0)),
        compiler_params=_params("parallel"),
        name="final_norm",
    )(x, gain.reshape(1, D_MODEL))


def _gla_kernel(*refs, chunk, sub, n_chunks, has_state):
    if has_state:
        (q_ref, k_ref, v_ref, g_ref, r_ref, wa2_ref, ba_ref, gain_ref, s0_ref,
         o_ref, sout_ref, s_ref) = refs
    else:
        (q_ref, k_ref, v_ref, g_ref, r_ref, wa2_ref, ba_ref, gain_ref,
         o_ref, sout_ref, s_ref) = refs
    t = pl.program_id(2)
    mx = BF16 if chunk >= 16 else F32

    @pl.when(t == 0)
    def _():
        if has_state:
            s_ref[...] = s0_ref[...]
        else:
            s_ref[...] = jnp.zeros_like(s_ref)

    row = lax.broadcasted_iota(jnp.int32, (chunk, chunk), 0)
    col = lax.broadcasted_iota(jnp.int32, (chunk, chunk), 1)
    tri = (col <= row).astype(mx)
    rowi = lax.broadcasted_iota(jnp.int32, (chunk, 1), 0)
    ones = jnp.ones((chunk, LANE), mx)

    s_old = s_ref[...]
    for c in range(n_chunks):
        r0 = c * chunk
        q = q_ref[r0:r0 + chunk, :].astype(F32) * (GLA_DK ** -0.5)
        k = k_ref[r0:r0 + chunk, :].astype(F32)
        v = v_ref[r0:r0 + chunk, :].astype(mx)
        z = _mm(r_ref[r0:r0 + chunk, :].astype(mx), wa2_ref[...].astype(mx)) + ba_ref[...]
        la = (jnp.minimum(z, 0.0) - jnp.log1p(jnp.exp(-jnp.abs(z)))) * GLA_INV_TAU
        if mx == BF16:
            la_hi, la_lo = _split2(la)
            b = _mm(tri, la_hi) + _mm(tri, la_lo)
            bcol = _tn(la_hi, ones) + _tn(la_lo, ones)
        else:
            b = _mmh(tri, la)
            bcol = lax.dot_general(la, ones, (((0,), (0,)), ((), ())),
                                   preferred_element_type=F32, precision=HIGHEST)
        o = _mm((q * jnp.exp(b)).astype(mx), s_old.astype(mx))
        b_last = b[chunk - 1:chunk, :]
        kb = (k * jnp.exp(b_last - b)).astype(mx)

        parts = []
        for band in range(chunk // sub):
            lo, hi = band * sub, (band + 1) * sub
            mid = lo + sub // 2
            bref = b[mid:mid + 1, :]
            qi = (q[lo:hi] * jnp.exp(b[lo:hi] - bref)).astype(mx)
            e = jnp.where(rowi < hi, bref - b, 0.0)
            ki = (k * jnp.exp(e)).astype(mx)
            parts.append(_nt(qi, ki))
        sc = parts[0] if len(parts) == 1 else jnp.concatenate(parts, axis=0)
        sc = jnp.where(col <= row, sc, 0.0)
        o = o + _mm(sc.astype(mx), v)

        s_old = jnp.exp(bcol[:, 0:1]) * s_old + _tn(kb, v)

        ms = jnp.mean(o * o, axis=-1, keepdims=True)
        y = o * lax.rsqrt(ms + EPS) * gain_ref[...]
        o_ref[r0:r0 + chunk, :] = (y * _silu(g_ref[r0:r0 + chunk, :].astype(F32))).astype(o_ref.dtype)
    s_ref[...] = s_old

    @pl.when(t == pl.num_programs(2) - 1)
    def _():
        sout_ref[...] = s_ref[...]


def _gla_core(proj, r, wa2p, b_a, gain, s0, *, bsz, length):
    chunk = min(GLA_CHUNK, length)
    sub = min(GLA_SUB, chunk)
    tl = min(256, length)
    grid = (bsz, GLA_HEADS, length // tl)
    in_specs = [
        pl.BlockSpec((None, tl, GLA_DK), lambda b, h, t: (b, t, h)),
        pl.BlockSpec((None, tl, GLA_DK), lambda b, h, t: (b, t, GLA_HEADS + h)),
        pl.BlockSpec((None, tl, GLA_DV), lambda b, h, t: (b, t, GLA_HEADS + h)),
        pl.BlockSpec((None, tl, GLA_DV), lambda b, h, t: (b, t, 2 * GLA_HEADS + h)),
        pl.BlockSpec((None, tl, LANE), lambda b, h, t: (b, t, 0)),
        pl.BlockSpec((LANE, GLA_DK), lambda b, h, t: (0, h)),
        pl.BlockSpec((1, GLA_DK), lambda b, h, t: (0, h)),
        pl.BlockSpec((1, GLA_DV), lambda b, h, t: (0, 0)),
    ]
    args = [proj, proj, proj, proj, r, wa2p, b_a, gain]
    if s0 is not None:
        in_specs.append(pl.BlockSpec((None, None, GLA_DK, GLA_DV), lambda b, h, t: (b, h, 0, 0)))
        args.append(s0)
    body = functools.partial(_gla_kernel, chunk=chunk, sub=sub, n_chunks=tl // chunk,
                             has_state=s0 is not None)
    return pl.pallas_call(
        body,
        out_shape=(jax.ShapeDtypeStruct((bsz, length, GLA_VW), BF16),
                   jax.ShapeDtypeStruct((bsz, GLA_HEADS, GLA_DK, GLA_DV), F32)),
        grid=grid,
        in_specs=in_specs,
        out_specs=(pl.BlockSpec((None, tl, GLA_DV), lambda b, h, t: (b, t, h)),
                   pl.BlockSpec((None, None, GLA_DK, GLA_DV), lambda b, h, t: (b, h, 0, 0))),
        scratch_shapes=[pltpu.VMEM((GLA_DK, GLA_DV), F32)],
        compiler_params=_params("parallel", "parallel", "arbitrary"),
        name="gla_scan",
    )(*args)


def _diff_lambda(lam_ref, lambda_init):
    lv = lam_ref[...]
    a = jnp.sum(lv[0:1] * lv[1:2], axis=-1, keepdims=True)
    b = jnp.sum(lv[2:3] * lv[3:4], axis=-1, keepdims=True)
    return jnp.exp(a) - jnp.exp(b) + lambda_init


def _diff_finish(o, gain_ref, lambda_init):
    ms = jnp.mean(o * o, axis=-1, keepdims=True)
    return o * lax.rsqrt(ms + EPS) * gain_ref[...] * (1.0 - lambda_init)


def _diff_prompt_kernel(q_ref, k_ref, v_ref, lam_ref, gain_ref, o_ref, m_ref, l_ref, acc_ref,
                        *, tq, tk, lambda_init):
    qi = pl.program_id(2)
    kj = pl.program_id(3)

    @pl.when(kj == 0)
    def _():
        m_ref[...] = jnp.full_like(m_ref, -jnp.inf)
        l_ref[...] = jnp.zeros_like(l_ref)
        acc_ref[...] = jnp.zeros_like(acc_ref)

    def attend(diagonal):
        q = (q_ref[...] * (DIFF_DH ** -0.5)).astype(BF16)
        k = k_ref[...].astype(BF16)
        v = v_ref[...].astype(BF16)
        for m in range(2):
            sl = slice(m * DIFF_DH, (m + 1) * DIFF_DH)
            s = _nt(q[:, sl], k[:, sl])
            if diagonal:
                keep = (lax.broadcasted_iota(jnp.int32, (tq, tk), 1)
                        <= lax.broadcasted_iota(jnp.int32, (tq, tk), 0))
                s = jnp.where(keep, s, -jnp.inf)
            m_old = m_ref[m]
            m_new = jnp.maximum(m_old, jnp.max(s, axis=-1, keepdims=True))
            alpha = jnp.exp(m_old - m_new)
            p = jnp.exp(s - _lane_tile(m_new, tk // LANE))
            l_ref[m] = alpha * l_ref[m] + jnp.sum(p, axis=-1, keepdims=True)
            acc_ref[m] = _lane_tile(alpha, 2 * DIFF_DH // LANE) * acc_ref[m] + _mm(p.astype(BF16), v)
            m_ref[m] = m_new

    @pl.when(kj < qi)
    def _():
        attend(False)

    @pl.when(kj == qi)
    def _():
        attend(True)
        lam = _diff_lambda(lam_ref, lambda_init)
        reps = 2 * DIFF_DH // LANE
        o = (acc_ref[0] / _lane_tile(l_ref[0], reps)
             - lam * (acc_ref[1] / _lane_tile(l_ref[1], reps)))
        o_ref[...] = _diff_finish(o, gain_ref, lambda_init).astype(o_ref.dtype)


def _diff_prompt(q, k, v, lam_vecs, gain, *, bsz, length, lambda_init):
    tq = tk = min(512, length)
    nq = length // tq
    hw = 2 * DIFF_DH
    body = functools.partial(_diff_prompt_kernel, tq=tq, tk=tk, lambda_init=lambda_init)
    return pl.pallas_call(
        body,
        out_shape=jax.ShapeDtypeStruct((bsz, length, DIFF_W), BF16),
        grid=(bsz, DIFF_HEADS, nq, nq),
        in_specs=[
            pl.BlockSpec((None, tq, hw), lambda b, h, i, j: (b, i, h)),
            pl.BlockSpec((None, tk, hw), lambda b, h, i, j: (b, jnp.minimum(j, i), h)),
            pl.BlockSpec((None, tk, hw), lambda b, h, i, j: (b, jnp.minimum(j, i), h)),
            pl.BlockSpec((4, DIFF_DH), lambda b, h, i, j: (0, 0)),
            pl.BlockSpec((1, hw), lambda b, h, i, j: (0, 0)),
        ],
        out_specs=pl.BlockSpec((None, tq, hw), lambda b, h, i, j: (b, i, h)),
        scratch_shapes=[pltpu.VMEM((2, tq, LANE), F32), pltpu.VMEM((2, tq, LANE), F32),
                        pltpu.VMEM((2, tq, hw), F32)],
        compiler_params=_params("parallel", "parallel", "parallel", "arbitrary"),
        name="diff_attn_prompt",
    )(q, k, v, lam_vecs, gain)


def _diff_decode_kernel(pt_ref, q_ref, *refs, n_steps, pps, n_new, lambda_init):
    del pt_ref
    kc_refs, vc_refs = refs[0:pps], refs[pps:2 * pps]
    (kn_ref, vn_ref, lam_ref, gain_ref, o_ref,
     qt_ref, bias_ref, m_ref, l_ref, acc_ref) = refs[2 * pps:]
    b = pl.program_id(0)
    j = pl.program_id(1)
    nrow = 2 * DIFF_HEADS * n_new
    hw = 2 * DIFF_DH
    page_cols = PAGE_SIZE * DIFF_HEADS

    def head_match(ncols):
        r = lax.broadcasted_iota(jnp.int32, (nrow, ncols), 0)
        c = lax.broadcasted_iota(jnp.int32, (nrow, ncols), 1)
        return r, c, (c % DIFF_HEADS) == ((r // n_new) % DIFF_HEADS)

    @pl.when((b == 0) & (j == 0))
    def _():
        _, _, same = head_match(page_cols)
        bias_ref[...] = jnp.where(same, 0.0, -jnp.inf)

    @pl.when(j == 0)
    def _():
        qt_ref[...] = jnp.zeros_like(qt_ref)
        for m in range(2):
            for h in range(DIFF_HEADS):
                r0 = (m * DIFF_HEADS + h) * n_new
                c0 = h * hw + m * DIFF_DH
                qt_ref[r0:r0 + n_new, m * DIFF_DH:(m + 1) * DIFF_DH] = (
                    q_ref[:, c0:c0 + DIFF_DH] * (DIFF_DH ** -0.5))
        m_ref[...] = jnp.full_like(m_ref, -jnp.inf)
        l_ref[...] = jnp.zeros_like(l_ref)
        acc_ref[...] = jnp.zeros_like(acc_ref)

    def attend(kvs, bias):
        qt = qt_ref[...].astype(BF16)
        scores = [_nt(qt, k2d.astype(BF16)) + bias for k2d, _ in kvs]
        ncols = bias.shape[1]
        m_old = m_ref[...]
        m_new = m_old
        for s in scores:
            m_new = jnp.maximum(m_new, jnp.max(s, axis=-1, keepdims=True))
        alpha = jnp.exp(m_old - m_new)
        l_new = alpha * l_ref[...]
        acc = _lane_tile(alpha, hw // LANE) * acc_ref[...]
        m_wide = m_new[:, 0:ncols] if ncols <= LANE else _lane_tile(m_new, ncols // LANE)
        for s, (_, v2d) in zip(scores, kvs):
            p = jnp.exp(s - m_wide)
            l_new = l_new + jnp.sum(p, axis=-1, keepdims=True)
            acc = acc + _mm(p.astype(BF16), v2d.astype(BF16))
        l_ref[...] = l_new
        acc_ref[...] = acc
        m_ref[...] = m_new

    @pl.when(j < n_steps)
    def _():
        attend([(kr[...].reshape(page_cols, hw), vr[...].reshape(page_cols, hw))
                for kr, vr in zip(kc_refs, vc_refs)], bias_ref[...])

    @pl.when(j == n_steps)
    def _():
        r, c, same = head_match(n_new * DIFF_HEADS)
        keep = same & ((c // DIFF_HEADS) <= (r % n_new))
        attend([(kn_ref[...], vn_ref[...])], jnp.where(keep, 0.0, -jnp.inf))
        lam = _diff_lambda(lam_ref, lambda_init)
        o2 = acc_ref[...] / _lane_tile(l_ref[...], hw // LANE)
        half = nrow // 2
        o = o2[0:half] - lam * o2[half:nrow]
        y = _diff_finish(o, gain_ref, lambda_init)
        for h in range(DIFF_HEADS):
            o_ref[:, h * hw:(h + 1) * hw] = y[h * n_new:(h + 1) * n_new, :].astype(o_ref.dtype)


def _diff_decode(q, k_new, v_new, cache_k, cache_v, page_table, layer, lam_vecs, gain,
                 *, bsz, n_new, lambda_init):
    n_pages = page_table.shape[1]
    pps = 4 if n_pages % 4 == 0 else (2 if n_pages % 2 == 0 else 1)
    n_steps = n_pages // pps
    hw = 2 * DIFF_DH
    nrow = 2 * DIFF_HEADS * n_new
    page_blk = (None, None, PAGE_SIZE, DIFF_HEADS, hw)

    def page_spec(p):
        return pl.BlockSpec(
            page_blk,
            lambda b, j, pt: (layer, pt[b, jnp.minimum(j, n_steps - 1) * pps + p], 0, 0, 0))

    body = functools.partial(_diff_decode_kernel, n_steps=n_steps, pps=pps, n_new=n_new,
                             lambda_init=lambda_init)
    new_spec = pl.BlockSpec((None, n_new * DIFF_HEADS, hw), lambda b, j, pt: (b, 0, 0))
    grid_spec = pltpu.PrefetchScalarGridSpec(
        num_scalar_prefetch=1,
        grid=(bsz, n_steps + 1),
        in_specs=(
            [pl.BlockSpec((None, n_new, DIFF_W), lambda b, j, pt: (b, 0, 0))]
            + [page_spec(p) for p in range(pps)] * 2
            + [new_spec, new_spec,
               pl.BlockSpec((4, DIFF_DH), lambda b, j, pt: (0, 0)),
               pl.BlockSpec((1, hw), lambda b, j, pt: (0, 0))]),
        out_specs=pl.BlockSpec((None, n_new, DIFF_W), lambda b, j, pt: (b, 0, 0)),
        scratch_shapes=[
            pltpu.VMEM((nrow, hw), F32),
            pltpu.VMEM((nrow, PAGE_SIZE * DIFF_HEADS), F32),
            pltpu.VMEM((nrow, LANE), F32),
            pltpu.VMEM((nrow, LANE), F32),
            pltpu.VMEM((nrow, hw), F32),
        ],
    )
    return pl.pallas_call(
        body,
        out_shape=jax.ShapeDtypeStruct((bsz, n_new, DIFF_W), BF16),
        grid_spec=grid_spec,
        compiler_params=_params("arbitrary", "arbitrary"),
        name="diff_attn_decode",
    )(page_table, q, *([cache_k] * pps), *([cache_v] * pps), k_new, v_new, lam_vecs, gain)


def _gdn_prep_kernel(x_ref, buf_ref, w_ref, o_ref, nb_ref, *, length, tc, nq_blocks, nqk_blocks):
    j = pl.program_id(1)
    x = x_ref[...].astype(F32)
    buf = buf_ref[...]
    w = w_ref[...]
    b0, b1, b2 = buf[0:1], buf[1:2], buf[2:3]
    head = 8

    def conv(x0, x1, x2, x3):
        return _silu(w[3:4] * x0 + w[2:3] * x1 + w[1:2] * x2 + w[0:1] * x3)

    def emit(rows, y):
        @pl.when(j < nqk_blocks)
        def _():
            scale = jnp.where(j < nq_blocks, GDN_DK ** -0.5, 1.0)
            for h in range(tc // GDN_DK):
                sl = slice(h * GDN_DK, (h + 1) * GDN_DK)
                yh = y[:, sl]
                ss = jnp.sum(yh * yh, axis=-1, keepdims=True)
                o_ref[rows, sl] = (yh * (lax.rsqrt(ss + EPS) * scale)).astype(o_ref.dtype)

        @pl.when(j >= nqk_blocks)
        def _():
            o_ref[rows, :] = y.astype(o_ref.dtype)

    if length > head:
        emit(slice(0, length),
             conv(x, pltpu.roll(x, 1, 0), pltpu.roll(x, 2, 0), pltpu.roll(x, 3, 0)))
    xt = x[0:head]
    row = lax.broadcasted_iota(jnp.int32, (head, 1), 0)
    x1 = jnp.where(row >= 1, pltpu.roll(xt, 1, 0), b2)
    x2 = jnp.where(row >= 2, pltpu.roll(xt, 2, 0), jnp.where(row == 1, b2, b1))
    x3 = jnp.where(row >= 3, pltpu.roll(xt, 3, 0),
                   jnp.where(row == 2, b2, jnp.where(row == 1, b1, b0)))
    emit(slice(0, head), conv(xt, x1, x2, x3))
    nb_ref[...] = x[length - (GDN_CONV - 1):length, :]


def _gdn_prep(proj, buf, conv_w, *, bsz, length):
    tc = 512
    body = functools.partial(_gdn_prep_kernel, length=length, tc=tc,
                             nq_blocks=GDN_KW // tc, nqk_blocks=2 * GDN_KW // tc)
    return pl.pallas_call(
        body,
        out_shape=(jax.ShapeDtypeStruct((bsz, length, GDN_QKV_W), BF16),
                   jax.ShapeDtypeStruct((bsz, GDN_CONV - 1, GDN_QKV_W), F32)),
        grid=(bsz, GDN_QKV_W // tc),
        in_specs=[
            pl.BlockSpec((None, length, tc), lambda b, j: (b, 0, j)),
            pl.BlockSpec((None, GDN_CONV - 1, tc), lambda b, j: (b, 0, j)),
            pl.BlockSpec((GDN_CONV, tc), lambda b, j: (0, j)),
        ],
        out_specs=(pl.BlockSpec((None, length, tc), lambda b, j: (b, 0, j)),
                   pl.BlockSpec((None, GDN_CONV - 1, tc), lambda b, j: (b, 0, j))),
        compiler_params=_params("parallel", "parallel"),
        name="gdn_conv_prep",
    )(proj, buf, conv_w)


def _gdn_kernel(*refs, chunk, n_chunks, has_state):
    if has_state:
        (q_ref, k_ref, v_ref, z_ref, bg_ref, alog_ref, dtb_ref, gain_ref, s0_ref,
         o_ref, sout_ref, s_ref) = refs
    else:
        (q_ref, k_ref, v_ref, z_ref, bg_ref, alog_ref, dtb_ref, gain_ref,
         o_ref, sout_ref, s_ref) = refs
    t = pl.program_id(2)
    big = chunk >= 16
    mx = BF16 if big else F32
    rep = GDN_GV // GDN_GQ

    @pl.when(t == 0)
    def _():
        if has_state:
            s_ref[...] = s0_ref[...]
        else:
            s_ref[...] = jnp.zeros_like(s_ref)

    row = lax.broadcasted_iota(jnp.int32, (chunk, chunk), 0)
    col = lax.broadcasted_iota(jnp.int32, (chunk, chunk), 1)
    lower = col <= row
    strict = col < row
    eye_f = (col == row).astype(F32)

    def bmm(a, b):
        return lax.dot_general(a, b, (((2,), (1,)), ((0,), (0,))), preferred_element_type=F32)

    def bmm_f32(a, b):
        return lax.dot_general(a, b, (((2,), (1,)), ((0,), (0,))),
                               preferred_element_type=F32, precision=HIGHEST)

    def bnt(a, b):
        return lax.dot_general(a, b, (((2,), (2,)), ((0,), (0,))), preferred_element_type=F32)

    heads = range(GDN_GV)

    def sub_block(size):
        return (((row // (2 * size)) == (col // (2 * size)))
                & ((row // size) % 2 == 1) & ((col // size) % 2 == 0))[None]

    def state_free(r0):
        rows = slice(r0, r0 + chunk)
        bg = bg_ref[rows, :]
        beta_all = jax.nn.sigmoid(bg[:, 0:GDN_GV])
        a_in = bg[:, GDN_GV:2 * GDN_GV] + dtb_ref[...]
        softplus = jnp.maximum(a_in, 0.0) + jnp.log1p(jnp.exp(-jnp.abs(a_in)))
        g_all = -jnp.exp(alog_ref[...]) * softplus
        if big:
            g1 = g_all.astype(BF16)
            r1 = g_all - g1.astype(F32)
            g2 = r1.astype(BF16)
            g3 = (r1 - g2.astype(F32)).astype(BF16)
            low_b = lower.astype(BF16)
            up_b = (row <= col).astype(BF16)
            gc_all = _mm(low_b, g1) + (_mm(low_b, g2) + _mm(low_b, g3))
            gr_all = _tn(g1, up_b) + (_tn(g2, up_b) + _tn(g3, up_b))
        else:
            gc_all = _mmh(lower.astype(F32), g_all)
            gr_all = lax.dot_general(g_all, (row <= col).astype(F32), (((0,), (0,)), ((), ())),
                                     preferred_element_type=F32, precision=HIGHEST)

        gcol = jnp.stack([gc_all[:, h:h + 1] for h in heads])
        grow = jnp.stack([gr_all[h:h + 1, :] for h in heads])
        beta = jnp.stack([beta_all[:, h:h + 1] for h in heads])
        q4 = jnp.stack([q_ref[rows, h * GDN_DK:(h + 1) * GDN_DK].astype(F32)
                        for h in range(GDN_GQ)])
        k4 = jnp.stack([k_ref[rows, h * GDN_DK:(h + 1) * GDN_DK].astype(F32)
                        for h in range(GDN_GQ)])
        kk4 = bnt(k4.astype(mx), k4.astype(mx))
        qk4 = bnt(q4.astype(mx), k4.astype(mx))
        kk = jnp.stack([kk4[h // rep] for h in heads])
        qk = jnp.stack([qk4[h // rep] for h in heads])
        k8 = jnp.stack([k4[h // rep] for h in heads])
        v8 = jnp.stack([v_ref[rows, h * GDN_DV:(h + 1) * GDN_DV].astype(F32) for h in heads])

        decay = jnp.exp(jnp.where(lower[None], gcol - grow, -jnp.inf))
        m = jnp.where(strict[None], beta * kk * decay, 0.0)

        eg = jnp.exp(gcol)
        rhs = jnp.concatenate([beta * v8, (beta * eg) * k8], axis=-1)
        inv = eye_f[None] - jnp.where(sub_block(1), m, 0.0)
        size = 2
        if big:
            m_b = m.astype(BF16)
            while size < chunk:
                low = jnp.where(sub_block(size), m_b, jnp.zeros_like(m_b))
                inv_b = inv.astype(BF16)
                inv = inv - bmm(bmm(inv_b, low).astype(BF16), inv_b)
                size *= 2
            inv_b = inv.astype(BF16)
            sol = bmm(inv_b, rhs.astype(BF16))
            s_hi, s_lo = _split2(sol)
            resid = rhs - sol - (bmm(m_b, s_hi) + bmm(m_b, s_lo))
            sol = sol + bmm(inv_b, resid.astype(BF16))
        else:
            while size < chunk:
                low = jnp.where(sub_block(size), m, 0.0)
                inv = inv - bmm_f32(bmm_f32(inv, low), inv)
                size *= 2
            sol = bmm_f32(inv, rhs)
        attn = jnp.where(lower[None], qk * decay, 0.0).astype(mx)
        g_last = gcol[:, chunk - 1:chunk, :]
        kd = (k8 * jnp.exp(g_last - gcol)).astype(mx)
        qe = (jnp.stack([q4[h // rep] for h in heads]) * eg).astype(mx)
        return sol, attn, kd, qe, jnp.exp(g_last)

    def recur(r0, parts, states):
        sol, attn, kd, qe, s_scale = parts
        rows = slice(r0, r0 + chunk)
        new_states = []
        for h in heads:
            vs = slice(h * GDN_DV, (h + 1) * GDN_DV)
            s_old = states[h]
            s_mx = s_old.astype(mx)
            delta = sol[h, :, 0:GDN_DV] - _mm(sol[h, :, GDN_DV:].astype(mx), s_mx)
            o = _mm(qe[h], s_mx) + _mm(attn[h], delta.astype(mx))
            new_states.append(s_scale[h] * s_old + _tn(kd[h], delta.astype(mx)))
            ms = jnp.mean(o * o, axis=-1, keepdims=True)
            y = o * lax.rsqrt(ms + EPS) * gain_ref[...]
            o_ref[rows, vs] = (y * _silu(z_ref[rows, vs].astype(F32))).astype(o_ref.dtype)
        return new_states

    parts = [state_free(c * chunk) for c in range(n_chunks)]
    states = [s_ref[h] for h in heads]
    for c in range(n_chunks):
        states = recur(c * chunk, parts[c], states)
    for h in heads:
        s_ref[h] = states[h]

    @pl.when(t == pl.num_programs(2) - 1)
    def _():
        sout_ref[...] = s_ref[...]


def _gdn_core(qkv, proj, bg, a_log, dt_bias, gain, s0, *, bsz, length):
    chunk = min(GDN_CHUNK, length)
    tl = min(4 * GDN_CHUNK, length)
    qw = GDN_GQ * GDN_DK
    vw = GDN_GV * GDN_DV
    in_specs = [
        pl.BlockSpec((None, tl, qw), lambda b, g, t: (b, t, g)),
        pl.BlockSpec((None, tl, qw), lambda b, g, t: (b, t, GDN_KW // qw + g)),
        pl.BlockSpec((None, tl, vw), lambda b, g, t: (b, t, 2 * GDN_KW // vw + g)),
        pl.BlockSpec((None, tl, vw), lambda b, g, t: (b, t, GDN_QKV_W // vw + g)),
        pl.BlockSpec((None, None, tl, 2 * GDN_GV), lambda b, g, t: (b, g, t, 0)),
        pl.BlockSpec((None, 1, GDN_GV), lambda b, g, t: (g, 0, 0)),
        pl.BlockSpec((None, 1, GDN_GV), lambda b, g, t: (g, 0, 0)),
        pl.BlockSpec((1, GDN_DV), lambda b, g, t: (0, 0)),
    ]
    args = [qkv, qkv, qkv, proj, bg, a_log, dt_bias, gain]
    if s0 is not None:
        in_specs.append(pl.BlockSpec((None, GDN_GV, GDN_DK, GDN_DV), lambda b, g, t: (b, g, 0, 0)))
        args.append(s0)
    body = functools.partial(_gdn_kernel, chunk=chunk, n_chunks=tl // chunk,
                             has_state=s0 is not None)
    return pl.pallas_call(
        body,
        out_shape=(jax.ShapeDtypeStruct((bsz, length, GDN_VW), BF16),
                   jax.ShapeDtypeStruct((bsz, GDN_V_HEADS, GDN_DK, GDN_DV), F32)),
        grid=(bsz, GDN_GROUPS, length // tl),
        in_specs=in_specs,
        out_specs=(pl.BlockSpec((None, tl, vw), lambda b, g, t: (b, t, g)),
                   pl.BlockSpec((None, GDN_GV, GDN_DK, GDN_DV), lambda b, g, t: (b, g, 0, 0))),
        scratch_shapes=[pltpu.VMEM((GDN_GV, GDN_DK, GDN_DV), F32)],
        compiler_params=_params("parallel", "parallel", "arbitrary"),
        name="gdn_scan",
    )(*args)


def _trunk(x_prompt, x_sample, w, past, mod, n_s):
    dims = [x_prompt.shape[0:2], x_sample.shape[0:2]]
    x = [x_prompt.reshape(-1, D_MODEL), x_sample.reshape(-1, D_MODEL)]
    seqs = [(n_s, dims[0][1]), (0, dims[1][1])]
    lin = functools.partial(_linear, seqs=seqs, mod=mod, tm=1024)
    groups = range(2)
    new_gla, new_dk, new_dv, new_gs, new_gc = ([[], []] for _ in range(5))

    def ffn(x, i, which, slots):
        sh, sc, g = slots
        gain = w['norm_gain'][i, 2 * which].reshape(1, D_MODEL)
        act = lin(x, w['ffn_w_up'], (i, which), n_out=D_FF, tn=512, out_dtype=BF16, layer=i,
                  norm=(gain, sh, sc), swiglu=True)
        return lin(act, w['ffn_w_down'], (i, which), n_out=D_MODEL, tn=256, out_dtype=F32,
                   layer=i, resid=(x, g, 0.5))

    for i in range(DEPTH):
        kind, j = i % 3, i // 3
        x = ffn(x, i, 0, (0, 1, 2))
        gain2 = w['norm_gain'][i, 1].reshape(1, D_MODEL)
        norm2 = (gain2, 3, 4)
        o = []
        if kind == 0:
            proj = lin(x, w['gla_w_in'], (j,), n_out=2 * GLA_KW + 2 * GLA_VW, tn=1024,
                       out_dtype=BF16, layer=i, norm=norm2)
            wa1p = jnp.pad(w['gla_w_a1'][j], ((0, 0), (0, LANE - GLA_RANK)))
            r = lin(x, wa1p, (), n_out=LANE, tn=LANE, out_dtype=F32, layer=i, norm=norm2)
            wa2p = jnp.pad(w['gla_w_a2'][j], ((0, LANE - GLA_RANK), (0, 0)))
            for g in groups:
                bsz, length = dims[g]
                s0 = None if g == 0 else past['state_gla'][j]
                og, s = _gla_core(proj[g].reshape(bsz, length, -1), r[g].reshape(bsz, length, LANE),
                                  wa2p, w['gla_b_a'][j].reshape(1, GLA_KW),
                                  w['gla_norm'][j].reshape(1, GLA_DV), s0, bsz=bsz, length=length)
                o.append(og)
                new_gla[g].append(s)
            w_o, w_o_pre = w['gla_w_o'], (j,)
        elif kind == 1:
            lambda_init = 0.8 - 0.6 * math.exp(-0.3 * i)
            qkv = lin(x, w['diff_w_in'], (j,), n_out=3 * DIFF_W, tn=512, out_dtype=F32,
                      layer=i, norm=norm2, n_split=3)
            gain = w['diff_norm'][j].reshape(1, 2 * DIFF_DH)
            for g in groups:
                bsz, length = dims[g]
                q, k, v = (t.reshape(bsz, length, DIFF_W) for t in qkv[g])
                if g == 0:
                    og = _diff_prompt(q, k, v, w['diff_lambda'][j], gain, bsz=bsz, length=length,
                                      lambda_init=lambda_init)
                else:
                    og = _diff_decode(q, k.reshape(bsz, length * DIFF_HEADS, 2 * DIFF_DH),
                                      v.reshape(bsz, length * DIFF_HEADS, 2 * DIFF_DH),
                                      past['cache_diff_k'], past['cache_diff_v'],
                                      past['page_table'], j, w['diff_lambda'][j], gain, bsz=bsz,
                                      n_new=length, lambda_init=lambda_init)
                o.append(og)
                new_dk[g].append(k.reshape(bsz, length, DIFF_HEADS, 2 * DIFF_DH))
                new_dv[g].append(v.reshape(bsz, length, DIFF_HEADS, 2 * DIFF_DH))
            w_o, w_o_pre = w['diff_w_o'], (j,)
        else:
            main_w = GDN_QKV_W + GDN_VW
            proj = lin(x, w['gdn_w_in'][j], (), n_out=main_w, tn=1024, out_dtype=BF16, layer=i,
                       norm=norm2)
            w_ba = jnp.pad(w['gdn_w_in'][j][:, main_w:], ((0, 0), (0, LANE - 2 * GDN_V_HEADS)))
            ba = lin(x, w_ba, (), n_out=LANE, tn=LANE, out_dtype=F32, layer=i, norm=norm2)
            for g in groups:
                bsz, length = dims[g]
                bg = ba[g].reshape(bsz, length, LANE)[:, :, 0:2 * GDN_V_HEADS]
                bg = bg.reshape(bsz, length, 2, GDN_GROUPS, GDN_GV)
                bg = jnp.transpose(bg, (0, 3, 1, 2, 4)).reshape(bsz, GDN_GROUPS, length, 2 * GDN_GV)
                proj3 = proj[g].reshape(bsz, length, main_w)
                if g == 0:
                    buf = jnp.zeros((bsz, GDN_CONV - 1, GDN_QKV_W), F32)
                    s0 = None
                else:
                    buf = past['state_gdn_conv'][j]
                    s0 = past['state_gdn'][j]
                qkv, nbuf = _gdn_prep(proj3, buf, w['gdn_conv_w'][j], bsz=bsz, length=length)
                og, s = _gdn_core(qkv, proj3, bg,
                                  w['gdn_a_log'][j].reshape(GDN_GROUPS, 1, GDN_GV),
                                  w['gdn_dt_bias'][j].reshape(GDN_GROUPS, 1, GDN_GV),
                                  w['gdn_norm'][j].reshape(1, GDN_DV), s0, bsz=bsz, length=length)
                o.append(og)
                new_gs[g].append(s)
                new_gc[g].append(nbuf)
            w_o, w_o_pre = w['gdn_w_o'], (j,)
        o = [o[g].reshape(x[g].shape[0], -1) for g in groups]
        tn_o = 1024 if w_o.shape[-2] <= D_MODEL else 512
        x = lin(o, w_o, w_o_pre, n_out=D_MODEL, tn=tn_o, out_dtype=F32, layer=i,
                resid=(x, 5, 1.0))
        x = ffn(x, i, 1, (6, 7, 8))
    y = [_final_norm(x[g], w['final_gain']).reshape(*dims[g], D_MODEL) for g in groups]
    return y, new_gla, new_dk, new_dv, new_gs, new_gc


def kernel(x_prompt, x_sample, cache_diff_k, cache_diff_v, state_gla, state_gdn, state_gdn_conv,
           page_table, c_prompt, c_sample, ada_w, ada_b, norm_gain, final_gain, ffn_w_up,
           ffn_w_down, gla_w_in, gla_w_a1, gla_w_a2, gla_b_a, gla_norm, gla_w_o, diff_w_in,
           diff_lambda, diff_norm, diff_w_o, gdn_w_in, gdn_conv_w, gdn_a_log, gdn_dt_bias,
           gdn_norm, gdn_w_o):
    w = {
        'norm_gain': norm_gain, 'final_gain': final_gain,
        'ffn_w_up': ffn_w_up, 'ffn_w_down': ffn_w_down,
        'gla_w_in': gla_w_in, 'gla_w_a1': gla_w_a1, 'gla_w_a2': gla_w_a2, 'gla_b_a': gla_b_a,
        'gla_norm': gla_norm, 'gla_w_o': gla_w_o,
        'diff_w_in': diff_w_in, 'diff_lambda': diff_lambda, 'diff_norm': diff_norm,
        'diff_w_o': diff_w_o,
        'gdn_w_in': gdn_w_in, 'gdn_conv_w': gdn_conv_w, 'gdn_a_log': gdn_a_log,
        'gdn_dt_bias': gdn_dt_bias, 'gdn_norm': gdn_norm, 'gdn_w_o': gdn_w_o,
    }
    n_s, n_p = c_sample.shape[0], c_prompt.shape[0]
    n_rows = 16
    c_all = jnp.concatenate(
        [c_sample, c_prompt, jnp.zeros((n_rows - n_s - n_p, D_MODEL), F32)], axis=0)
    mod = _ada(c_all, ada_w, ada_b).reshape(DEPTH, n_rows, 1, N_MOD * D_MODEL)

    past = {
        'state_gla': state_gla, 'cache_diff_k': cache_diff_k, 'cache_diff_v': cache_diff_v,
        'page_table': page_table, 'state_gdn': state_gdn, 'state_gdn_conv': state_gdn_conv,
    }
    y, gla, dk, dv, gdn, conv = _trunk(x_prompt, x_sample, w, past, mod, n_s)
    return (y[0], y[1],
            jnp.stack(gla[0]), jnp.stack(gla[1]),
            jnp.stack(dk[0]), jnp.stack(dv[0]), jnp.stack(dk[1]), jnp.stack(dv[1]),
            jnp.stack(gdn[0]), jnp.stack(gdn[1]),
            jnp.stack(conv[0]), jnp.stack(conv[1]))
```

```python
import functools
import math

import jax
import jax.numpy as jnp
from jax import lax
from jax.experimental import pallas as pl
from jax.experimental.pallas import tpu as pltpu

F32 = jnp.float32
BF16 = jnp.bfloat16
HIGHEST = lax.Precision.HIGHEST

D_MODEL = 2048
DEPTH = 4
EPS = 1e-6
N_MOD = 9
D_FF = 5632

GLA_HEADS = 4
GLA_DK = 256
GLA_DV = 512
GLA_KW = GLA_HEADS * GLA_DK
GLA_VW = GLA_HEADS * GLA_DV
GLA_RANK = 16
GLA_INV_TAU = 1.0 / 16.0
GLA_CHUNK = 64
GLA_SUB = 16

DIFF_DH = 128
DIFF_HEADS = 8
DIFF_W = 2048
PAGE_SIZE = 128

GDN_DK = 128
GDN_DV = 128
GDN_QK_HEADS = 16
GDN_V_HEADS = 32
GDN_KW = 2048
GDN_VW = 4096
GDN_QKV_W = 8192
GDN_CONV = 4
GDN_CHUNK = 64
GDN_GROUPS = 4
GDN_GQ = GDN_QK_HEADS // GDN_GROUPS
GDN_GV = GDN_V_HEADS // GDN_GROUPS

LANE = 128
VMEM_LIMIT = 56 * 1024 * 1024


def _params(*sem):
    return pltpu.CompilerParams(dimension_semantics=sem, vmem_limit_bytes=VMEM_LIMIT)


def _nt(a, b):
    return lax.dot_general(a, b, (((1,), (1,)), ((), ())), preferred_element_type=F32)


def _tn(a, b):
    return lax.dot_general(a, b, (((0,), (0,)), ((), ())), preferred_element_type=F32)


def _mm(a, b):
    return jnp.dot(a, b, preferred_element_type=F32)


def _mmh(a, b):
    return jnp.dot(a, b, preferred_element_type=F32, precision=HIGHEST)


def _silu(x):
    return x * jax.nn.sigmoid(x)


def _lane_tile(x, n):
    return x if n == 1 else jnp.concatenate([x] * n, axis=1)


def _split2(x):
    hi = x.astype(BF16)
    lo = (x - hi.astype(F32)).astype(BF16)
    return hi, lo


def _ada_kernel(c_ref, w_ref, b_ref, o_ref):
    s = _silu(c_ref[...]).astype(BF16)
    o_ref[...] = _mm(s, w_ref[...].astype(BF16)) + b_ref[...]


def _ada(c_all, ada_w, ada_b):
    rows = c_all.shape[0]
    width = ada_w.shape[2]
    tn = 1024
    return pl.pallas_call(
        _ada_kernel,
        out_shape=jax.ShapeDtypeStruct((DEPTH, rows, width), F32),
        grid=(DEPTH, width // tn),
        in_specs=[
            pl.BlockSpec((rows, D_MODEL), lambda l, j: (0, 0)),
            pl.BlockSpec((None, D_MODEL, tn), lambda l, j: (l, 0, j)),
            pl.BlockSpec((None, 1, tn), lambda l, j: (l, 0, j)),
        ],
        out_specs=pl.BlockSpec((None, rows, tn), lambda l, j: (l, 0, j)),
        compiler_params=_params("parallel", "parallel"),
        name="ada_mod",
    )(c_all, ada_w, ada_b.reshape(DEPTH, 1, width))


def _linear_kernel(*refs, has_norm, n_seq, swiglu, has_resid, coef, n_split, cols_per_split):
    it = iter(refs)
    x_refs, sh_refs, sc_refs = [], [], []
    for _ in range(2):
        x_refs.append(next(it))
        if has_norm:
            sh_refs.append(next(it))
            sc_refs.append(next(it))
    gain_ref = next(it) if has_norm else None
    w_ref = next(it)
    wu_ref = next(it) if swiglu else None
    res_refs, gate_refs = [], []
    if has_resid:
        for _ in range(2):
            res_refs.append(next(it))
            gate_refs.append(next(it))
    o_refs = [[next(it) for _ in range(n_split)] for _ in range(2)]
    h_refs = [next(it), next(it)] if has_norm else None
    i = pl.program_id(0)
    j = pl.program_id(1)

    def run(g):
        def per_seq(val, vec_ref, fn):
            if n_seq[g] == 1:
                return fn(val, vec_ref[0])
            rows, n = val.shape
            return fn(val.reshape(n_seq[g], rows // n_seq[g], n), vec_ref[...]).reshape(rows, n)

        if has_norm:
            @pl.when(j == 0)
            def _():
                xf = x_refs[g][...].astype(F32)
                ms = jnp.mean(xf * xf, axis=-1, keepdims=True)
                y = xf * lax.rsqrt(ms + EPS) * gain_ref[...]
                y = per_seq(y, sc_refs[g], lambda a, s: a * (1.0 + s))
                y = per_seq(y, sh_refs[g], lambda a, s: a + s)
                h_refs[g][...] = y.astype(BF16)
            h = h_refs[g][...]
        else:
            h = x_refs[g][...].astype(BF16)

        acc = _mm(h, w_ref[...].astype(BF16))
        if swiglu:
            up = _mm(h, wu_ref[...].astype(BF16))
            acc = _silu(acc) * up
        if has_resid:
            acc = per_seq(acc, gate_refs[g], lambda a, gt: a * (coef * gt))
            acc = res_refs[g][...] + acc
        if n_split == 1:
            o_refs[g][0][...] = acc.astype(o_refs[g][0].dtype)
        else:
            for s in range(n_split):
                @pl.when(j // cols_per_split == s)
                def _():
                    o_refs[g][s][...] = acc.astype(o_refs[g][s].dtype)

    run(0)

    @pl.when(i == pl.num_programs(0) - 1)
    def _():
        run(1)


def _linear(xs, w, w_prefix, *, n_out, tm, tn, out_dtype, seqs, mod, layer,
            norm=None, swiglu=False, resid=None, n_split=1):
    m0, k_dim = xs[0].shape
    m1 = xs[1].shape[0]
    tm = min(tm, m0)
    n_i = m0 // tm
    nj = n_out // tn
    cols_per_split = nj // n_split
    seq0_a, len_a = seqs[0]
    seq0_b, len_b = seqs[1]
    n_seq = (max(1, tm // len_a), m1 // len_b)
    tiles_per_seq = max(1, len_a // tm)
    npre = len(w_prefix)

    def seq_blk(g, i):
        if g == 1:
            return seq0_b // n_seq[1]
        if n_seq[0] == 1:
            return seq0_a + i // tiles_per_seq
        return seq0_a // n_seq[0] + i

    def col(g, i, c):
        return c if g == 0 else jnp.where(i == n_i - 1, c, 0)

    def mod_spec(g, slot, width, per_j):
        blocks = D_MODEL // width
        if per_j:
            return pl.BlockSpec((None, n_seq[g], 1, width),
                                lambda i, j: (layer, seq_blk(g, i), 0, slot * blocks + col(g, i, j)))
        return pl.BlockSpec((None, n_seq[g], 1, width),
                            lambda i, j: (layer, seq_blk(g, i), 0, slot * blocks))

    def row_spec(g, width, per_j):
        rows = tm if g == 0 else m1
        if per_j:
            return pl.BlockSpec((rows, width), lambda i, j: (i if g == 0 else 0, col(g, i, j)))
        return pl.BlockSpec((rows, width), lambda i, j: (i if g == 0 else 0, 0))

    def out_spec(g, s):
        rows = tm if g == 0 else m1
        return pl.BlockSpec(
            (rows, tn),
            lambda i, j: (i if g == 0 else 0,
                          col(g, i, jnp.clip(j - s * cols_per_split, 0, cols_per_split - 1))))

    def w_spec(col0):
        return pl.BlockSpec((None,) * npre + (k_dim, tn),
                            lambda i, j: tuple(w_prefix) + (0, col0 + j))

    args, in_specs, scratch = [], [], []
    for g in range(2):
        args.append(xs[g])
        in_specs.append(row_spec(g, k_dim, False))
        if norm is not None:
            args += [mod, mod]
            in_specs += [mod_spec(g, norm[1], D_MODEL, False), mod_spec(g, norm[2], D_MODEL, False)]
    if norm is not None:
        args.append(norm[0])
        in_specs.append(pl.BlockSpec((1, k_dim), lambda i, j: (0, 0)))
        scratch += [pltpu.VMEM((tm, k_dim), BF16), pltpu.VMEM((m1, k_dim), BF16)]
    args.append(w)
    in_specs.append(w_spec(0))
    if swiglu:
        args.append(w)
        in_specs.append(w_spec(nj))
    coef = 1.0
    if resid is not None:
        res, g_slot, coef = resid
        for g in range(2):
            args += [res[g], mod]
            in_specs += [row_spec(g, tn, True), mod_spec(g, g_slot, tn, True)]

    body = functools.partial(_linear_kernel, has_norm=norm is not None, n_seq=n_seq,
                             swiglu=swiglu, has_resid=resid is not None, coef=coef,
                             n_split=n_split, cols_per_split=cols_per_split)
    width = n_out // n_split
    outs = pl.pallas_call(
        body,
        out_shape=[jax.ShapeDtypeStruct((m, width), out_dtype)
                   for m in (m0, m1) for _ in range(n_split)],
        grid=(n_i, nj),
        in_specs=in_specs,
        out_specs=[out_spec(g, s) for g in range(2) for s in range(n_split)],
        scratch_shapes=scratch,
        compiler_params=_params("arbitrary", "arbitrary"),
        name="linear",
    )(*args)
    if n_split == 1:
        return [outs[0], outs[1]]
    return [outs[0:n_split], outs[n_split:2 * n_split]]


def _rmsnorm_kernel(x_ref, g_ref, o_ref):
    x = x_ref[...]
    ms = jnp.mean(x * x, axis=-1, keepdims=True)
    o_ref[...] = x * lax.rsqrt(ms + EPS) * g_ref[...]


def _final_norm(x, gain):
    m_rows = x.shape[0]
    tm = min(512, m_rows)
    return pl.pallas_call(
        _rmsnorm_kernel,
        out_shape=jax.ShapeDtypeStruct(x.shape, F32),
        grid=(m_rows // tm,),
        in_specs=[pl.BlockSpec((tm, D_MODEL), lambda i: (i, 0)),
                  pl.BlockSpec((1, D_MODEL), lambda i: (0, 0))],
        out_specs=pl.BlockSpec((tm, D_MODEL), lambda i: (i, 0)),
        compiler_params=_params("parallel"),
        name="final_norm",
    )(x, gain.reshape(1, D_MODEL))


def _gla_kernel(*refs, chunk, sub, n_chunks, has_state):
    if has_state:
        (q_ref, k_ref, v_ref, g_ref, r_ref, wa2_ref, ba_ref, gain_ref, s0_ref,
         o_ref, sout_ref, s_ref) = refs
    else:
        (q_ref, k_ref, v_ref, g_ref, r_ref, wa2_ref, ba_ref, gain_ref,
         o_ref, sout_ref, s_ref) = refs
    t = pl.program_id(2)
    mx = BF16 if chunk >= 16 else F32

    @pl.when(t == 0)
    def _():
        if has_state:
            s_ref[...] = s0_ref[...]
        else:
            s_ref[...] = jnp.zeros_like(s_ref)

    row = lax.broadcasted_iota(jnp.int32, (chunk, chunk), 0)
    col = lax.broadcasted_iota(jnp.int32, (chunk, chunk), 1)
    tri = (col <= row).astype(mx)
    rowi = lax.broadcasted_iota(jnp.int32, (chunk, 1), 0)
    ones = jnp.ones((chunk, LANE), mx)

    s_old = s_ref[...]
    for c in range(n_chunks):
        r0 = c * chunk
        q = q_ref[r0:r0 + chunk, :].astype(F32) * (GLA_DK ** -0.5)
        k = k_ref[r0:r0 + chunk, :].astype(F32)
        v = v_ref[r0:r0 + chunk, :].astype(mx)
        z = _mm(r_ref[r0:r0 + chunk, :].astype(mx), wa2_ref[...].astype(mx)) + ba_ref[...]
        la = (jnp.minimum(z, 0.0) - jnp.log1p(jnp.exp(-jnp.abs(z)))) * GLA_INV_TAU
        if mx == BF16:
            la_hi, la_lo = _split2(la)
            b = _mm(tri, la_hi) + _mm(tri, la_lo)
            bcol = _tn(la_hi, ones) + _tn(la_lo, ones)
        else:
            b = _mmh(tri, la)
            bcol = lax.dot_general(la, ones, (((0,), (0,)), ((), ())),
                                   preferred_element_type=F32, precision=HIGHEST)
        o = _mm((q * jnp.exp(b)).astype(mx), s_old.astype(mx))
        b_last = b[chunk - 1:chunk, :]
        kb = (k * jnp.exp(b_last - b)).astype(mx)

        parts = []
        for band in range(chunk // sub):
            lo, hi = band * sub, (band + 1) * sub
            mid = lo + sub // 2
            bref = b[mid:mid + 1, :]
            qi = (q[lo:hi] * jnp.exp(b[lo:hi] - bref)).astype(mx)
            e = jnp.where(rowi < hi, bref - b, 0.0)
            ki = (k * jnp.exp(e)).astype(mx)
            parts.append(_nt(qi, ki))
        sc = parts[0] if len(parts) == 1 else jnp.concatenate(parts, axis=0)
        sc = jnp.where(col <= row, sc, 0.0)
        o = o + _mm(sc.astype(mx), v)

        s_old = jnp.exp(bcol[:, 0:1]) * s_old + _tn(kb, v)

        ms = jnp.mean(o * o, axis=-1, keepdims=True)
        y = o * lax.rsqrt(ms + EPS) * gain_ref[...]
        o_ref[r0:r0 + chunk, :] = (y * _silu(g_ref[r0:r0 + chunk, :].astype(F32))).astype(o_ref.dtype)
    s_ref[...] = s_old

    @pl.when(t == pl.num_programs(2) - 1)
    def _():
        sout_ref[...] = s_ref[...]


def _gla_core(proj, r, wa2p, b_a, gain, s0, *, bsz, length):
    chunk = min(GLA_CHUNK, length)
    sub = min(GLA_SUB, chunk)
    tl = min(512, length)
    grid = (bsz, GLA_HEADS, length // tl)
    in_specs = [
        pl.BlockSpec((None, tl, GLA_DK), lambda b, h, t: (b, t, h)),
        pl.BlockSpec((None, tl, GLA_DK), lambda b, h, t: (b, t, GLA_HEADS + h)),
        pl.BlockSpec((None, tl, GLA_DV), lambda b, h, t: (b, t, GLA_HEADS + h)),
        pl.BlockSpec((None, tl, GLA_DV), lambda b, h, t: (b, t, 2 * GLA_HEADS + h)),
        pl.BlockSpec((None, tl, LANE), lambda b, h, t: (b, t, 0)),
        pl.BlockSpec((LANE, GLA_DK), lambda b, h, t: (0, h)),
        pl.BlockSpec((1, GLA_DK), lambda b, h, t: (0, h)),
        pl.BlockSpec((1, GLA_DV), lambda b, h, t: (0, 0)),
    ]
    args = [proj, proj, proj, proj, r, wa2p, b_a, gain]
    if s0 is not None:
        in_specs.append(pl.BlockSpec((None, None, GLA_DK, GLA_DV), lambda b, h, t: (b, h, 0, 0)))
        args.append(s0)
    body = functools.partial(_gla_kernel, chunk=chunk, sub=sub, n_chunks=tl // chunk,
                             has_state=s0 is not None)
    return pl.pallas_call(
        body,
        out_shape=(jax.ShapeDtypeStruct((bsz, length, GLA_VW), BF16),
                   jax.ShapeDtypeStruct((bsz, GLA_HEADS, GLA_DK, GLA_DV), F32)),
        grid=grid,
        in_specs=in_specs,
        out_specs=(pl.BlockSpec((None, tl, GLA_DV), lambda b, h, t: (b, t, h)),
                   pl.BlockSpec((None, None, GLA_DK, GLA_DV), lambda b, h, t: (b, h, 0, 0))),
        scratch_shapes=[pltpu.VMEM((GLA_DK, GLA_DV), F32)],
        compiler_params=_params("parallel", "parallel", "arbitrary"),
        name="gla_scan",
    )(*args)


def _diff_lambda(lam_ref, lambda_init):
    lv = lam_ref[...]
    a = jnp.sum(lv[0:1] * lv[1:2], axis=-1, keepdims=True)
    b = jnp.sum(lv[2:3] * lv[3:4], axis=-1, keepdims=True)
    return jnp.exp(a) - jnp.exp(b) + lambda_init


def _diff_finish(o, gain_ref, lambda_init):
    ms = jnp.mean(o * o, axis=-1, keepdims=True)
    return o * lax.rsqrt(ms + EPS) * gain_ref[...] * (1.0 - lambda_init)


def _diff_prompt_kernel(q_ref, k_ref, v_ref, lam_ref, gain_ref, o_ref, m_ref, l_ref, acc_ref,
                        *, tq, tk, lambda_init):
    qi = pl.program_id(2)
    kj = pl.program_id(3)

    @pl.when(kj == 0)
    def _():
        m_ref[...] = jnp.full_like(m_ref, -jnp.inf)
        l_ref[...] = jnp.zeros_like(l_ref)
        acc_ref[...] = jnp.zeros_like(acc_ref)

    def attend(diagonal):
        q = (q_ref[...] * (DIFF_DH ** -0.5)).astype(BF16)
        k = k_ref[...].astype(BF16)
        v = v_ref[...].astype(BF16)
        for m in range(2):
            sl = slice(m * DIFF_DH, (m + 1) * DIFF_DH)
            s = _nt(q[:, sl], k[:, sl])
            if diagonal:
                keep = (lax.broadcasted_iota(jnp.int32, (tq, tk), 1)
                        <= lax.broadcasted_iota(jnp.int32, (tq, tk), 0))
                s = jnp.where(keep, s, -jnp.inf)
            m_old = m_ref[m]
            m_new = jnp.maximum(m_old, jnp.max(s, axis=-1, keepdims=True))
            alpha = jnp.exp(m_old - m_new)
            p = jnp.exp(s - _lane_tile(m_new, tk // LANE))
            l_ref[m] = alpha * l_ref[m] + jnp.sum(p, axis=-1, keepdims=True)
            acc_ref[m] = _lane_tile(alpha, 2 * DIFF_DH // LANE) * acc_ref[m] + _mm(p.astype(BF16), v)
            m_ref[m] = m_new

    @pl.when(kj < qi)
    def _():
        attend(False)

    @pl.when(kj == qi)
    def _():
        attend(True)
        lam = _diff_lambda(lam_ref, lambda_init)
        reps = 2 * DIFF_DH // LANE
        o = (acc_ref[0] / _lane_tile(l_ref[0], reps)
             - lam * (acc_ref[1] / _lane_tile(l_ref[1], reps)))
        o_ref[...] = _diff_finish(o, gain_ref, lambda_init).astype(o_ref.dtype)


def _diff_prompt(q, k, v, lam_vecs, gain, *, bsz, length, lambda_init):
    tq = tk = min(512, length)
    nq = length // tq
    hw = 2 * DIFF_DH
    body = functools.partial(_diff_prompt_kernel, tq=tq, tk=tk, lambda_init=lambda_init)
    return pl.pallas_call(
        body,
        out_shape=jax.ShapeDtypeStruct((bsz, length, DIFF_W), BF16),
        grid=(bsz, DIFF_HEADS, nq, nq),
        in_specs=[
            pl.BlockSpec((None, tq, hw), lambda b, h, i, j: (b, i, h)),
            pl.BlockSpec((None, tk, hw), lambda b, h, i, j: (b, jnp.minimum(j, i), h)),
            pl.BlockSpec((None, tk, hw), lambda b, h, i, j: (b, jnp.minimum(j, i), h)),
            pl.BlockSpec((4, DIFF_DH), lambda b, h, i, j: (0, 0)),
            pl.BlockSpec((1, hw), lambda b, h, i, j: (0, 0)),
        ],
        out_specs=pl.BlockSpec((None, tq, hw), lambda b, h, i, j: (b, i, h)),
        scratch_shapes=[pltpu.VMEM((2, tq, LANE), F32), pltpu.VMEM((2, tq, LANE), F32),
                        pltpu.VMEM((2, tq, hw), F32)],
        compiler_params=_params("parallel", "parallel", "parallel", "arbitrary"),
        name="diff_attn_prompt",
    )(q, k, v, lam_vecs, gain)


def _diff_decode_kernel(pt_ref, q_ref, *refs, n_steps, pps, n_new, lambda_init):
    del pt_ref
    kc_refs, vc_refs = refs[0:pps], refs[pps:2 * pps]
    (kn_ref, vn_ref, lam_ref, gain_ref, o_ref,
     qt_ref, bias_ref, m_ref, l_ref, acc_ref) = refs[2 * pps:]
    b = pl.program_id(0)
    j = pl.program_id(1)
    nrow = 2 * DIFF_HEADS * n_new
    hw = 2 * DIFF_DH
    page_cols = PAGE_SIZE * DIFF_HEADS

    def head_match(ncols):
        r = lax.broadcasted_iota(jnp.int32, (nrow, ncols), 0)
        c = lax.broadcasted_iota(jnp.int32, (nrow, ncols), 1)
        return r, c, (c % DIFF_HEADS) == ((r // n_new) % DIFF_HEADS)

    @pl.when((b == 0) & (j == 0))
    def _():
        _, _, same = head_match(page_cols)
        bias_ref[...] = jnp.where(same, 0.0, -jnp.inf)

    @pl.when(j == 0)
    def _():
        qt_ref[...] = jnp.zeros_like(qt_ref)
        for m in range(2):
            for h in range(DIFF_HEADS):
                r0 = (m * DIFF_HEADS + h) * n_new
                c0 = h * hw + m * DIFF_DH
                qt_ref[r0:r0 + n_new, m * DIFF_DH:(m + 1) * DIFF_DH] = (
                    q_ref[:, c0:c0 + DIFF_DH] * (DIFF_DH ** -0.5))
        m_ref[...] = jnp.full_like(m_ref, -jnp.inf)
        l_ref[...] = jnp.zeros_like(l_ref)
        acc_ref[...] = jnp.zeros_like(acc_ref)

    def attend(kvs, bias):
        qt = qt_ref[...].astype(BF16)
        scores = [_nt(qt, k2d.astype(BF16)) + bias for k2d, _ in kvs]
        ncols = bias.shape[1]
        m_old = m_ref[...]
        m_new = m_old
        for s in scores:
            m_new = jnp.maximum(m_new, jnp.max(s, axis=-1, keepdims=True))
        alpha = jnp.exp(m_old - m_new)
        l_new = alpha * l_ref[...]
        acc = _lane_tile(alpha, hw // LANE) * acc_ref[...]
        m_wide = m_new[:, 0:ncols] if ncols <= LANE else _lane_tile(m_new, ncols // LANE)
        for s, (_, v2d) in zip(scores, kvs):
            p = jnp.exp(s - m_wide)
            l_new = l_new + jnp.sum(p, axis=-1, keepdims=True)
            acc = acc + _mm(p.astype(BF16), v2d.astype(BF16))
        l_ref[...] = l_new
        acc_ref[...] = acc
        m_ref[...] = m_new

    @pl.when(j < n_steps)
    def _():
        attend([(kr[...].reshape(page_cols, hw), vr[...].reshape(page_cols, hw))
                for kr, vr in zip(kc_refs, vc_refs)], bias_ref[...])

    @pl.when(j == n_steps)
    def _():
        r, c, same = head_match(n_new * DIFF_HEADS)
        keep = same & ((c // DIFF_HEADS) <= (r % n_new))
        attend([(kn_ref[...], vn_ref[...])], jnp.where(keep, 0.0, -jnp.inf))
        lam = _diff_lambda(lam_ref, lambda_init)
        o2 = acc_ref[...] / _lane_tile(l_ref[...], hw // LANE)
        half = nrow // 2
        o = o2[0:half] - lam * o2[half:nrow]
        y = _diff_finish(o, gain_ref, lambda_init)
        for h in range(DIFF_HEADS):
            o_ref[:, h * hw:(h + 1) * hw] = y[h * n_new:(h + 1) * n_new, :].astype(o_ref.dtype)


def _diff_decode(q, k_new, v_new, cache_k, cache_v, page_table, layer, lam_vecs, gain,
                 *, bsz, n_new, lambda_init):
    n_pages = page_table.shape[1]
    pps = next(p for p in (8, 4, 2, 1) if n_pages % p == 0)
    n_steps = n_pages // pps
    hw = 2 * DIFF_DH
    nrow = 2 * DIFF_HEADS * n_new
    page_blk = (None, None, PAGE_SIZE, DIFF_HEADS, hw)

    def page_spec(p):
        return pl.BlockSpec(
            page_blk,
            lambda b, j, pt: (layer, pt[b, jnp.minimum(j, n_steps - 1) * pps + p], 0, 0, 0))

    body = functools.partial(_diff_decode_kernel, n_steps=n_steps, pps=pps, n_new=n_new,
                             lambda_init=lambda_init)
    new_spec = pl.BlockSpec((None, n_new * DIFF_HEADS, hw), lambda b, j, pt: (b, 0, 0))
    grid_spec = pltpu.PrefetchScalarGridSpec(
        num_scalar_prefetch=1,
        grid=(bsz, n_steps + 1),
        in_specs=(
            [pl.BlockSpec((None, n_new, DIFF_W), lambda b, j, pt: (b, 0, 0))]
            + [page_spec(p) for p in range(pps)] * 2
            + [new_spec, new_spec,
               pl.BlockSpec((4, DIFF_DH), lambda b, j, pt: (0, 0)),
               pl.BlockSpec((1, hw), lambda b, j, pt: (0, 0))]),
        out_specs=pl.BlockSpec((None, n_new, DIFF_W), lambda b, j, pt: (b, 0, 0)),
        scratch_shapes=[
            pltpu.VMEM((nrow, hw), F32),
            pltpu.VMEM((nrow, PAGE_SIZE * DIFF_HEADS), F32),
            pltpu.VMEM((nrow, LANE), F32),
            pltpu.VMEM((nrow, LANE), F32),
            pltpu.VMEM((nrow, hw), F32),
        ],
    )
    return pl.pallas_call(
        body,
        out_shape=jax.ShapeDtypeStruct((bsz, n_new, DIFF_W), BF16),
        grid_spec=grid_spec,
        compiler_params=_params("arbitrary", "arbitrary"),
        name="diff_attn_decode",
    )(page_table, q, *([cache_k] * pps), *([cache_v] * pps), k_new, v_new, lam_vecs, gain)


def _gdn_prep_kernel(x_ref, buf_ref, w_ref, o_ref, nb_ref, *, length, tc, nq_blocks, nqk_blocks):
    j = pl.program_id(1)
    x = x_ref[...].astype(F32)
    buf = buf_ref[...]
    w = w_ref[...]
    b0, b1, b2 = buf[0:1], buf[1:2], buf[2:3]
    head = 8

    def conv(x0, x1, x2, x3):
        return _silu(w[3:4] * x0 + w[2:3] * x1 + w[1:2] * x2 + w[0:1] * x3)

    def emit(rows, y):
        @pl.when(j < nqk_blocks)
        def _():
            scale = jnp.where(j < nq_blocks, GDN_DK ** -0.5, 1.0)
            for h in range(tc // GDN_DK):
                sl = slice(h * GDN_DK, (h + 1) * GDN_DK)
                yh = y[:, sl]
                ss = jnp.sum(yh * yh, axis=-1, keepdims=True)
                o_ref[rows, sl] = (yh * (lax.rsqrt(ss + EPS) * scale)).astype(o_ref.dtype)

        @pl.when(j >= nqk_blocks)
        def _():
            o_ref[rows, :] = y.astype(o_ref.dtype)

    if length > head:
        emit(slice(0, length),
             conv(x, pltpu.roll(x, 1, 0), pltpu.roll(x, 2, 0), pltpu.roll(x, 3, 0)))
    xt = x[0:head]
    row = lax.broadcasted_iota(jnp.int32, (head, 1), 0)
    x1 = jnp.where(row >= 1, pltpu.roll(xt, 1, 0), b2)
    x2 = jnp.where(row >= 2, pltpu.roll(xt, 2, 0), jnp.where(row == 1, b2, b1))
    x3 = jnp.where(row >= 3, pltpu.roll(xt, 3, 0),
                   jnp.where(row == 2, b2, jnp.where(row == 1, b1, b0)))
    emit(slice(0, head), conv(xt, x1, x2, x3))
    nb_ref[...] = x[length - (GDN_CONV - 1):length, :]


def _gdn_prep(proj, buf, conv_w, *, bsz, length):
    tc = 512 if length >= 512 else GDN_KW
    body = functools.partial(_gdn_prep_kernel, length=length, tc=tc,
                             nq_blocks=GDN_KW // tc, nqk_blocks=2 * GDN_KW // tc)
    return pl.pallas_call(
        body,
        out_shape=(jax.ShapeDtypeStruct((bsz, length, GDN_QKV_W), BF16),
                   jax.ShapeDtypeStruct((bsz, GDN_CONV - 1, GDN_QKV_W), F32)),
        grid=(bsz, GDN_QKV_W // tc),
        in_specs=[
            pl.BlockSpec((None, length, tc), lambda b, j: (b, 0, j)),
            pl.BlockSpec((None, GDN_CONV - 1, tc), lambda b, j: (b, 0, j)),
            pl.BlockSpec((GDN_CONV, tc), lambda b, j: (0, j)),
        ],
        out_specs=(pl.BlockSpec((None, length, tc), lambda b, j: (b, 0, j)),
                   pl.BlockSpec((None, GDN_CONV - 1, tc), lambda b, j: (b, 0, j))),
        compiler_params=_params("parallel", "parallel"),
        name="gdn_conv_prep",
    )(proj, buf, conv_w)


def _gdn_kernel(*refs, chunk, n_chunks, has_state):
    if has_state:
        (q_ref, k_ref, v_ref, z_ref, bg_ref, alog_ref, dtb_ref, gain_ref, s0_ref,
         o_ref, sout_ref, s_ref) = refs
    else:
        (q_ref, k_ref, v_ref, z_ref, bg_ref, alog_ref, dtb_ref, gain_ref,
         o_ref, sout_ref, s_ref) = refs
    t = pl.program_id(2)
    big = chunk >= 16
    mx = BF16 if big else F32
    rep = GDN_GV // GDN_GQ

    @pl.when(t == 0)
    def _():
        if has_state:
            s_ref[...] = s0_ref[...]
        else:
            s_ref[...] = jnp.zeros_like(s_ref)

    row = lax.broadcasted_iota(jnp.int32, (chunk, chunk), 0)
    col = lax.broadcasted_iota(jnp.int32, (chunk, chunk), 1)
    lower = col <= row
    strict = col < row
    eye_f = (col == row).astype(F32)

    def bmm(a, b):
        return lax.dot_general(a, b, (((2,), (1,)), ((0,), (0,))), preferred_element_type=F32)

    def bmm_f32(a, b):
        return lax.dot_general(a, b, (((2,), (1,)), ((0,), (0,))),
                               preferred_element_type=F32, precision=HIGHEST)

    def bnt(a, b):
        return lax.dot_general(a, b, (((2,), (2,)), ((0,), (0,))), preferred_element_type=F32)

    heads = range(GDN_GV)

    def sub_block(size):
        return (((row // (2 * size)) == (col // (2 * size)))
                & ((row // size) % 2 == 1) & ((col // size) % 2 == 0))[None]

    def state_free(r0):
        rows = slice(r0, r0 + chunk)
        bg = bg_ref[rows, :]
        beta_all = jax.nn.sigmoid(bg[:, 0:GDN_GV])
        a_in = bg[:, GDN_GV:2 * GDN_GV] + dtb_ref[...]
        softplus = jnp.maximum(a_in, 0.0) + jnp.log1p(jnp.exp(-jnp.abs(a_in)))
        g_all = -jnp.exp(alog_ref[...]) * softplus
        if big:
            g1 = g_all.astype(BF16)
            r1 = g_all - g1.astype(F32)
            g2 = r1.astype(BF16)
            g3 = (r1 - g2.astype(F32)).astype(BF16)
            low_b = lower.astype(BF16)
            up_b = (row <= col).astype(BF16)
            gc_all = _mm(low_b, g1) + (_mm(low_b, g2) + _mm(low_b, g3))
            gr_all = _tn(g1, up_b) + (_tn(g2, up_b) + _tn(g3, up_b))
        else:
            gc_all = _mmh(lower.astype(F32), g_all)
            gr_all = lax.dot_general(g_all, (row <= col).astype(F32), (((0,), (0,)), ((), ())),
                                     preferred_element_type=F32, precision=HIGHEST)

        gcol = jnp.stack([gc_all[:, h:h + 1] for h in heads])
        grow = jnp.stack([gr_all[h:h + 1, :] for h in heads])
        beta = jnp.stack([beta_all[:, h:h + 1] for h in heads])
        q4 = jnp.stack([q_ref[rows, h * GDN_DK:(h + 1) * GDN_DK].astype(F32)
                        for h in range(GDN_GQ)])
        k4 = jnp.stack([k_ref[rows, h * GDN_DK:(h + 1) * GDN_DK].astype(F32)
                        for h in range(GDN_GQ)])
        kk4 = bnt(k4.astype(mx), k4.astype(mx))
        qk4 = bnt(q4.astype(mx), k4.astype(mx))
        kk = jnp.stack([kk4[h // rep] for h in heads])
        qk = jnp.stack([qk4[h // rep] for h in heads])
        k8 = jnp.stack([k4[h // rep] for h in heads])
        v8 = jnp.stack([v_ref[rows, h * GDN_DV:(h + 1) * GDN_DV].astype(F32) for h in heads])

        decay = jnp.exp(jnp.where(lower[None], gcol - grow, -jnp.inf))
        m = jnp.where(strict[None], beta * kk * decay, 0.0)

        eg = jnp.exp(gcol)
        rhs = jnp.concatenate([beta * v8, (beta * eg) * k8], axis=-1)
        inv = eye_f[None] - jnp.where(sub_block(1), m, 0.0)
        size = 2
        if big:
            m_b = m.astype(BF16)
            while size < chunk:
                low = jnp.where(sub_block(size), m_b, jnp.zeros_like(m_b))
                inv_b = inv.astype(BF16)
                inv = inv - bmm(bmm(inv_b, low).astype(BF16), inv_b)
                size *= 2
            inv_b = inv.astype(BF16)
            sol = bmm(inv_b, rhs.astype(BF16))
            s_hi, s_lo = _split2(sol)
            resid = rhs - sol - (bmm(m_b, s_hi) + bmm(m_b, s_lo))
            sol = sol + bmm(inv_b, resid.astype(BF16))
        else:
            while size < chunk:
                low = jnp.where(sub_block(size), m, 0.0)
                inv = inv - bmm_f32(bmm_f32(inv, low), inv)
                size *= 2
            sol = bmm_f32(inv, rhs)
        attn = jnp.where(lower[None], qk * decay, 0.0).astype(mx)
        g_last = gcol[:, chunk - 1:chunk, :]
        kd = (k8 * jnp.exp(g_last - gcol)).astype(mx)
        qe = (jnp.stack([q4[h // rep] for h in heads]) * eg).astype(mx)
        return sol, attn, kd, qe, jnp.exp(g_last)

    def recur(r0, parts, states):
        sol, attn, kd, qe, s_scale = parts
        rows = slice(r0, r0 + chunk)
        new_states = []
        for h in heads:
            vs = slice(h * GDN_DV, (h + 1) * GDN_DV)
            s_old = states[h]
            s_mx = s_old.astype(mx)
            delta = sol[h, :, 0:GDN_DV] - _mm(sol[h, :, GDN_DV:].astype(mx), s_mx)
            o = _mm(qe[h], s_mx) + _mm(attn[h], delta.astype(mx))
            new_states.append(s_scale[h] * s_old + _tn(kd[h], delta.astype(mx)))
            ms = jnp.mean(o * o, axis=-1, keepdims=True)
            y = o * lax.rsqrt(ms + EPS) * gain_ref[...]
            o_ref[rows, vs] = (y * _silu(z_ref[rows, vs].astype(F32))).astype(o_ref.dtype)
        return new_states

    parts = [state_free(c * chunk) for c in range(n_chunks)]
    states = [s_ref[h] for h in heads]
    for c in range(n_chunks):
        states = recur(c * chunk, parts[c], states)
    for h in heads:
        s_ref[h] = states[h]

    @pl.when(t == pl.num_programs(2) - 1)
    def _():
        sout_ref[...] = s_ref[...]


def _gdn_core(qkv, proj, bg, a_log, dt_bias, gain, s0, *, bsz, length):
    chunk = min(GDN_CHUNK, length)
    tl = min(4 * GDN_CHUNK, length)
    qw = GDN_GQ * GDN_DK
    vw = GDN_GV * GDN_DV
    in_specs = [
        pl.BlockSpec((None, tl, qw), lambda b, g, t: (b, t, g)),
        pl.BlockSpec((None, tl, qw), lambda b, g, t: (b, t, GDN_KW // qw + g)),
        pl.BlockSpec((None, tl, vw), lambda b, g, t: (b, t, 2 * GDN_KW // vw + g)),
        pl.BlockSpec((None, tl, vw), lambda b, g, t: (b, t, GDN_QKV_W // vw + g)),
        pl.BlockSpec((None, None, tl, 2 * GDN_GV), lambda b, g, t: (b, g, t, 0)),
        pl.BlockSpec((None, 1, GDN_GV), lambda b, g, t: (g, 0, 0)),
        pl.BlockSpec((None, 1, GDN_GV), lambda b, g, t: (g, 0, 0)),
        pl.BlockSpec((1, GDN_DV), lambda b, g, t: (0, 0)),
    ]
    args = [qkv, qkv, qkv, proj, bg, a_log, dt_bias, gain]
    if s0 is not None:
        in_specs.append(pl.BlockSpec((None, GDN_GV, GDN_DK, GDN_DV), lambda b, g, t: (b, g, 0, 0)))
        args.append(s0)
    body = functools.partial(_gdn_kernel, chunk=chunk, n_chunks=tl // chunk,
                             has_state=s0 is not None)
    return pl.pallas_call(
        body,
        out_shape=(jax.ShapeDtypeStruct((bsz, length, GDN_VW), BF16),
                   jax.ShapeDtypeStruct((bsz, GDN_V_HEADS, GDN_DK, GDN_DV), F32)),
        grid=(bsz, GDN_GROUPS, length // tl),
        in_specs=in_specs,
        out_specs=(pl.BlockSpec((None, tl, vw), lambda b, g, t: (b, t, g)),
                   pl.BlockSpec((None, GDN_GV, GDN_DK, GDN_DV), lambda b, g, t: (b, g, 0, 0))),
        scratch_shapes=[pltpu.VMEM((GDN_GV, GDN_DK, GDN_DV), F32)],
        compiler_params=_params("parallel", "parallel", "arbitrary"),
        name="gdn_scan",
    )(*args)


def _trunk(x_prompt, x_sample, w, past, mod, n_s):
    dims = [x_prompt.shape[0:2], x_sample.shape[0:2]]
    x = [x_prompt.reshape(-1, D_MODEL), x_sample.reshape(-1, D_MODEL)]
    seqs = [(n_s, dims[0][1]), (0, dims[1][1])]
    lin = functools.partial(_linear, seqs=seqs, mod=mod, tm=1024)
    groups = range(2)
    new_gla, new_dk, new_dv, new_gs, new_gc = ([[], []] for _ in range(5))

    def ffn(x, i, which, slots):
        sh, sc, g = slots
        gain = w['norm_gain'][i, 2 * which].reshape(1, D_MODEL)
        act = lin(x, w['ffn_w_up'], (i, which), n_out=D_FF, tn=512, out_dtype=BF16, layer=i,
                  norm=(gain, sh, sc), swiglu=True)
        return lin(act, w['ffn_w_down'], (i, which), n_out=D_MODEL, tn=256, out_dtype=F32,
                   layer=i, resid=(x, g, 0.5))

    for i in range(DEPTH):
        kind, j = i % 3, i // 3
        x = ffn(x, i, 0, (0, 1, 2))
        gain2 = w['norm_gain'][i, 1].reshape(1, D_MODEL)
        norm2 = (gain2, 3, 4)
        o = []
        if kind == 0:
            proj = lin(x, w['gla_w_in'], (j,), n_out=2 * GLA_KW + 2 * GLA_VW, tn=1024,
                       out_dtype=BF16, layer=i, norm=norm2)
            wa1p = jnp.pad(w['gla_w_a1'][j], ((0, 0), (0, LANE - GLA_RANK)))
            r = lin(x, wa1p, (), n_out=LANE, tn=LANE, out_dtype=F32, layer=i, norm=norm2)
            wa2p = jnp.pad(w['gla_w_a2'][j], ((0, LANE - GLA_RANK), (0, 0)))
            for g in groups:
                bsz, length = dims[g]
                s0 = None if g == 0 else past['state_gla'][j]
                og, s = _gla_core(proj[g].reshape(bsz, length, -1), r[g].reshape(bsz, length, LANE),
                                  wa2p, w['gla_b_a'][j].reshape(1, GLA_KW),
                                  w['gla_norm'][j].reshape(1, GLA_DV), s0, bsz=bsz, length=length)
                o.append(og)
                new_gla[g].append(s)
            w_o, w_o_pre = w['gla_w_o'], (j,)
        elif kind == 1:
            lambda_init = 0.8 - 0.6 * math.exp(-0.3 * i)
            qkv = lin(x, w['diff_w_in'], (j,), n_out=3 * DIFF_W, tn=512, out_dtype=F32,
                      layer=i, norm=norm2, n_split=3)
            gain = w['diff_norm'][j].reshape(1, 2 * DIFF_DH)
            for g in groups:
                bsz, length = dims[g]
                q, k, v = (t.reshape(bsz, length, DIFF_W) for t in qkv[g])
                if g == 0:
                    og = _diff_prompt(q, k, v, w['diff_lambda'][j], gain, bsz=bsz, length=length,
                                      lambda_init=lambda_init)
                else:
                    og = _diff_decode(q, k.reshape(bsz, length * DIFF_HEADS, 2 * DIFF_DH),
                                      v.reshape(bsz, length * DIFF_HEADS, 2 * DIFF_DH),
                                      past['cache_diff_k'], past['cache_diff_v'],
                                      past['page_table'], j, w['diff_lambda'][j], gain, bsz=bsz,
                                      n_new=length, lambda_init=lambda_init)
                o.append(og)
                new_dk[g].append(k.reshape(bsz, length, DIFF_HEADS, 2 * DIFF_DH))
                new_dv[g].append(v.reshape(bsz, length, DIFF_HEADS, 2 * DIFF_DH))
            w_o, w_o_pre = w['diff_w_o'], (j,)
        else:
            main_w = GDN_QKV_W + GDN_VW
            proj = lin(x, w['gdn_w_in'][j], (), n_out=main_w, tn=1024, out_dtype=BF16, layer=i,
                       norm=norm2)
            w_ba = jnp.pad(w['gdn_w_in'][j][:, main_w:], ((0, 0), (0, LANE - 2 * GDN_V_HEADS)))
            ba = lin(x, w_ba, (), n_out=LANE, tn=LANE, out_dtype=F32, layer=i, norm=norm2)
            for g in groups:
                bsz, length = dims[g]
                bg = ba[g].reshape(bsz, length, LANE)[:, :, 0:2 * GDN_V_HEADS]
                bg = bg.reshape(bsz, length, 2, GDN_GROUPS, GDN_GV)
                bg = jnp.transpose(bg, (0, 3, 1, 2, 4)).reshape(bsz, GDN_GROUPS, length, 2 * GDN_GV)
                proj3 = proj[g].reshape(bsz, length, main_w)
                if g == 0:
                    buf = jnp.zeros((bsz, GDN_CONV - 1, GDN_QKV_W), F32)
                    s0 = None
                else:
                    buf = past['state_gdn_conv'][j]
                    s0 = past['state_gdn'][j]
                qkv, nbuf = _gdn_prep(proj3, buf, w['gdn_conv_w'][j], bsz=bsz, length=length)
                og, s = _gdn_core(qkv, proj3, bg,
                                  w['gdn_a_log'][j].reshape(GDN_GROUPS, 1, GDN_GV),
                                  w['gdn_dt_bias'][j].reshape(GDN_GROUPS, 1, GDN_GV),
                                  w['gdn_norm'][j].reshape(1, GDN_DV), s0, bsz=bsz, length=length)
                o.append(og)
                new_gs[g].append(s)
                new_gc[g].append(nbuf)
            w_o, w_o_pre = w['gdn_w_o'], (j,)
        o = [o[g].reshape(x[g].shape[0], -1) for g in groups]
        tn_o = 1024 if w_o.shape[-2] <= D_MODEL else 512
        x = lin(o, w_o, w_o_pre, n_out=D_MODEL, tn=tn_o, out_dtype=F32, layer=i,
                resid=(x, 5, 1.0))
        x = ffn(x, i, 1, (6, 7, 8))
    y = [_final_norm(x[g], w['final_gain']).reshape(*dims[g], D_MODEL) for g in groups]
    return y, new_gla, new_dk, new_dv, new_gs, new_gc


def kernel(x_prompt, x_sample, cache_diff_k, cache_diff_v, state_gla, state_gdn, state_gdn_conv,
           page_table, c_prompt, c_sample, ada_w, ada_b, norm_gain, final_gain, ffn_w_up,
           ffn_w_down, gla_w_in, gla_w_a1, gla_w_a2, gla_b_a, gla_norm, gla_w_o, diff_w_in,
           diff_lambda, diff_norm, diff_w_o, gdn_w_in, gdn_conv_w, gdn_a_log, gdn_dt_bias,
           gdn_norm, gdn_w_o):
    w = {
        'norm_gain': norm_gain, 'final_gain': final_gain,
        'ffn_w_up': ffn_w_up, 'ffn_w_down': ffn_w_down,
        'gla_w_in': gla_w_in, 'gla_w_a1': gla_w_a1, 'gla_w_a2': gla_w_a2, 'gla_b_a': gla_b_a,
        'gla_norm': gla_norm, 'gla_w_o': gla_w_o,
        'diff_w_in': diff_w_in, 'diff_lambda': diff_lambda, 'diff_norm': diff_norm,
        'diff_w_o': diff_w_o,
        'gdn_w_in': gdn_w_in, 'gdn_conv_w': gdn_conv_w, 'gdn_a_log': gdn_a_log,
        'gdn_dt_bias': gdn_dt_bias, 'gdn_norm': gdn_norm, 'gdn_w_o': gdn_w_o,
    }
    n_s, n_p = c_sample.shape[0], c_prompt.shape[0]
    n_rows = 16
    c_all = jnp.concatenate(
        [c_sample, c_prompt, jnp.zeros((n_rows - n_s - n_p, D_MODEL), F32)], axis=0)
    mod = _ada(c_all, ada_w, ada_b).reshape(DEPTH, n_rows, 1, N_MOD * D_MODEL)

    past = {
        'state_gla': state_gla, 'cache_diff_k': cache_diff_k, 'cache_diff_v': cache_diff_v,
        'page_table': page_table, 'state_gdn': state_gdn, 'state_gdn_conv': state_gdn_conv,
    }
    y, gla, dk, dv, gdn, conv = _trunk(x_prompt, x_sample, w, past, mod, n_s)
    return (y[0], y[1],
            jnp.stack(gla[0]), jnp.stack(gla[1]),
            jnp.stack(dk[0]), jnp.stack(dv[0]), jnp.stack(dk[1]), jnp.stack(dv[1]),
            jnp.stack(gdn[0]), jnp.stack(gdn[1]),
            jnp.stack(conv[0]), jnp.stack(conv[1]))
```
